```python
import math
import jax, jax.numpy as jnp
from jax import lax
import numpy as np

D_MODEL = 1024
BATCH = 16
SEQ = 256
DEPTH = 4
DEC_BATCH = 8
DEC_SEQ = 2048
PAST_LEN = 256

GRID_W = 64
MIX = D_MODEL
SSD_INNER = MIX // 2
SSD_HEAD_DIM = 64
SSD_HEADS = SSD_INNER // SSD_HEAD_DIM
SSD_GROUPS = 2
SSD_STATE = 128
SSD_CONV = 5
SSD_CHUNK = 128
SSD_CONV_CH = SSD_INNER + 2 * SSD_GROUPS * SSD_STATE
SC_WIDTH = MIX // 4
SC_CONV = 3
DIFF_WIDTH = MIX // 4
DIFF_HEADS = 4
DIFF_HD = DIFF_WIDTH // DIFF_HEADS // 2
DIFF_VD = 2 * DIFF_HD
Q_BLOCK = 128
ROPE_BASE = 10000.0
ROPE_F = DIFF_HD // 4
D_FF = 4 * D_MODEL
IN_COLS = SSD_INNER + SSD_CONV_CH + 2 * SSD_HEADS + 3 * SC_WIDTH + 3 * DIFF_WIDTH
ALPHA = (2 * DEPTH) ** 0.25
BETA = (8 * DEPTH) ** -0.25
LN_EPS = 1e-5
RMS_EPS = 1e-6

kernel_name = "hybrid_ssd_conv_diffattn_dit_step"


def _in_split_points():
    sizes = (SSD_INNER, SSD_CONV_CH, 2 * SSD_HEADS, SC_WIDTH, SC_WIDTH, SC_WIDTH, DIFF_WIDTH, DIFF_WIDTH)
    pts = []
    acc = 0
    for s in sizes:
        acc += s
        pts.append(acc)
    return pts


def layer_norm(x, g, b):
    xf = x.astype(jnp.float32)
    mu = jnp.mean(xf, axis=-1, keepdims=True)
    var = jnp.mean(jnp.square(xf - mu), axis=-1, keepdims=True)
    return ((xf - mu) * lax.rsqrt(var + LN_EPS) * g.astype(jnp.float32) + b.astype(jnp.float32)).astype(x.dtype)


def rms_norm(x, g):
    xf = x.astype(jnp.float32)
    y = xf * lax.rsqrt(jnp.mean(jnp.square(xf), axis=-1, keepdims=True) + RMS_EPS)
    return (y * g.astype(jnp.float32)).astype(x.dtype)


def dwconv(x, w):
    width = w.shape[0]
    return lax.conv_general_dilated(
        x, w[:, None, :].astype(x.dtype), window_strides=(1,),
        padding=[(width // 2, width // 2)], dimension_numbers=('NWC', 'WIO', 'NWC'),
        feature_group_count=x.shape[-1])


def axial_angles(rows):
    row = jnp.repeat(jnp.arange(rows, dtype=jnp.float32), GRID_W)
    col = jnp.tile(jnp.arange(GRID_W, dtype=jnp.float32), rows)
    inv = ROPE_BASE ** (-jnp.arange(ROPE_F, dtype=jnp.float32) / ROPE_F)
    return jnp.stack([row[:, None] * inv, col[:, None] * inv], axis=1)


def rope2d(x, theta):
    b, n, h, d = x.shape
    xf = x.astype(jnp.float32).reshape(b, n, h, 2, 2, ROPE_F)
    x1, x2 = xf[..., 0, :], xf[..., 1, :]
    cos = jnp.cos(theta)[None, :, None]
    sin = jnp.sin(theta)[None, :, None]
    out = jnp.stack([x1 * cos - x2 * sin, x2 * cos + x1 * sin], axis=-2)
    return out.reshape(b, n, h, d).astype(x.dtype)


def diff_attention(q, k, v, lam):
    b, n = q.shape[0], q.shape[1]
    nb = n // Q_BLOCK
    qh = jnp.moveaxis(q.reshape(b, nb, Q_BLOCK, DIFF_HEADS, 2, DIFF_HD), 1, 0)
    kh = k.reshape(b, k.shape[1], DIFF_HEADS, 2, DIFF_HD)
    scale = DIFF_HD ** -0.5

    def block(qb):
        s = jnp.einsum('bqhsd,bkhsd->bhsqk', qb, kh).astype(jnp.float32) * scale
        pr = jax.nn.softmax(s, axis=-1)
        pd = pr[:, :, 0] - lam * pr[:, :, 1]
        return jnp.einsum('bhqk,bkhv->bqhv', pd.astype(v.dtype), v)

    o = lax.map(block, qh)
    return jnp.moveaxis(o, 0, 1).reshape(b, n, DIFF_HEADS, DIFF_VD)


def ssd_scan(x, dt, a, bm, cm, s0):
    b, n = x.shape[0], x.shape[1]
    nc = n // SSD_CHUNK
    e = SSD_HEADS // SSD_GROUPS
    xs = x.astype(jnp.float32).reshape(b, nc, SSD_CHUNK, SSD_GROUPS, e, SSD_HEAD_DIM)
    dts = dt.astype(jnp.float32).reshape(b, nc, SSD_CHUNK, SSD_GROUPS, e)
    bs = bm.astype(jnp.float32).reshape(b, nc, SSD_CHUNK, SSD_GROUPS, SSD_STATE)
    cs = cm.astype(jnp.float32).reshape(b, nc, SSD_CHUNK, SSD_GROUPS, SSD_STATE)
    acum = jnp.cumsum(dts * a.astype(jnp.float32).reshape(SSD_GROUPS, e), axis=2)
    xdt = xs * dts[..., None]
    mask = jnp.tril(jnp.ones((SSD_CHUNK, SSD_CHUNK), dtype=bool))
    seg = acum[:, :, :, None] - acum[:, :, None, :]
    decay = jnp.exp(jnp.where(mask[:, :, None, None], seg, -jnp.inf))
    cb = jnp.einsum('bcign,bcjgn->bcijg', cs, bs)
    y_diag = jnp.einsum('bcijge,bcjgep->bcigep', cb[..., None] * decay, xdt)
    decay_to_end = jnp.exp(acum[:, :, -1:] - acum)
    states = jnp.einsum('bcjgn,bcjge,bcjgep->bcgepn', bs, decay_to_end, xdt)
    chunk_decay = jnp.exp(acum[:, :, -1])

    def step(s, inp):
        st, dec = inp
        return dec[..., None, None] * s + st, s

    s_init = s0.astype(jnp.float32).reshape(b, SSD_GROUPS, e, SSD_HEAD_DIM, SSD_STATE)
    s_fin, s_prev = lax.scan(step, s_init, (jnp.moveaxis(states, 1, 0), jnp.moveaxis(chunk_decay, 1, 0)))
    s_prev = jnp.moveaxis(s_prev, 0, 1)
    y_off = jnp.einsum('bcign,bcgepn,bcige->bcigep', cs, s_prev, jnp.exp(acum))
    y = (y_diag + y_off).reshape(b, n, SSD_HEADS, SSD_HEAD_DIM)
    return y, s_fin.reshape(b, SSD_HEADS, SSD_HEAD_DIM, SSD_STATE)


def mixer(u, p, lam_init, theta, prefix_k, prefix_v, s0_f, s0_b):
    b, n = u.shape[0], u.shape[1]
    proj = u @ p['w_in']
    z, xbc, dt, sc_b, sc_c, sc_h, q, k, v = jnp.split(proj, _in_split_points(), axis=-1)

    xbc = jax.nn.silu(dwconv(xbc, p['ssd_conv_w']) + p['ssd_conv_b'])
    xs, bm, cm = jnp.split(xbc, [SSD_INNER, SSD_INNER + SSD_GROUPS * SSD_STATE], axis=-1)
    xs = xs.reshape(b, n, SSD_HEADS, SSD_HEAD_DIM)
    bm = bm.reshape(b, n, SSD_GROUPS, SSD_STATE)
    cm = cm.reshape(b, n, SSD_GROUPS, SSD_STATE)
    dt = jax.nn.softplus(dt.astype(jnp.float32).reshape(b, n, 2, SSD_HEADS) + p['ssd_dt_bias'].astype(jnp.float32))
    a = -jnp.exp(p['ssd_a_log'].astype(jnp.float32))
    flip = lambda t: jnp.flip(t, axis=1)
    y_f, s_f = ssd_scan(xs, dt[:, :, 0], a[0], bm, cm, s0_f)
    y_b, s_b = ssd_scan(flip(xs), flip(dt[:, :, 1]), a[1], flip(bm), flip(cm), s0_b)
    y_ssd = y_f + flip(y_b) + p['ssd_d'].astype(jnp.float32)[:, None] * xs.astype(jnp.float32)
    y_ssd = y_ssd.reshape(b, n, SSD_INNER) * jax.nn.silu(z.astype(jnp.float32))
    y_ssd = rms_norm(y_ssd, p['ssd_norm_w']).astype(u.dtype)

    y_sc = sc_b * dwconv(sc_c * sc_h, p['sc_conv_w'])

    q = q.reshape(b, n, 2 * DIFF_HEADS, DIFF_HD)
    k = k.reshape(b, n, 2 * DIFF_HEADS, DIFF_HD)
    v = v.reshape(b, n, DIFF_HEADS, DIFF_VD)
    if theta is None:
        keys, vals = k, v
    else:
        q = rope2d(q, theta)
        k = rope2d(k, theta)
        keys = jnp.concatenate([prefix_k.astype(k.dtype), k], axis=1)
        vals = jnp.concatenate([prefix_v.astype(v.dtype), v], axis=1)
    lp = p['diff_lambda'].astype(jnp.float32)
    lam = jnp.exp(jnp.sum(lp[0] * lp[1])) - jnp.exp(jnp.sum(lp[2] * lp[3])) + lam_init
    o = diff_attention(q, keys, vals, lam)
    o = rms_norm(o, p['diff_norm_w']) * (1.0 - lam_init)
    y_attn = o.reshape(b, n, DIFF_WIDTH).astype(u.dtype)

    mixed = jnp.concatenate([y_ssd, y_sc, y_attn], axis=-1) @ p['w_out']
    return mixed, k, v, s_f.astype(u.dtype), s_b.astype(u.dtype)


def trunk_layer(x, mod, p, lam_init, theta, prefix_k, prefix_v, s0_f, s0_b):
    sh1, sc1, g1, sh2, sc2, g2 = jnp.split(mod, 6, axis=-1)
    u = x * (1.0 + sc1) + sh1
    m, k, v, s_f, s_b = mixer(u, p, lam_init, theta, prefix_k, prefix_v, s0_f, s0_b)
    x = layer_norm(ALPHA * x + g1 * m, p['ln1_g'], p['ln1_b'])
    h = x * (1.0 + sc2) + sh2
    f = jnp.square(jax.nn.relu(h @ p['w_up'])) @ p['w_down']
    x = layer_norm(ALPHA * x + g2 * f, p['ln2_g'], p['ln2_b'])
    return x, k, v, s_f, s_b


def setup_inputs(seed: int = 0) -> dict:
    key = jax.random.key(seed)
    ks = jax.random.split(key, 32)
    nrm = lambda k_, shape, s: jax.random.normal(k_, shape, jnp.float32) * s
    u_dt = jax.random.uniform(ks[12], (DEPTH, 2, SSD_HEADS), jnp.float32)
    dt0 = jnp.exp(u_dt * (math.log(0.1) - math.log(0.001)) + math.log(0.001))
    return {
        'x_prompt': nrm(ks[0], (BATCH, SEQ, D_MODEL), 1.0),
        'x_sample': nrm(ks[1], (DEC_BATCH, DEC_SEQ, D_MODEL), 1.0),
        'cache_k': nrm(ks[2], (DEC_BATCH, DEPTH, PAST_LEN, 2 * DIFF_HEADS, DIFF_HD), 1.0),
        'cache_v': nrm(ks[3], (DEC_BATCH, DEPTH, PAST_LEN, DIFF_HEADS, DIFF_VD), 1.0),
        'state_ssm_fwd': nrm(ks[4], (DEC_BATCH, DEPTH, SSD_HEADS, SSD_HEAD_DIM, SSD_STATE), 0.1),
        'state_ssm_bwd': nrm(ks[5], (DEC_BATCH, DEPTH, SSD_HEADS, SSD_HEAD_DIM, SSD_STATE), 0.1),
        'c': nrm(ks[6], (DEC_BATCH, D_MODEL), 1.0),
        'c_ctx': nrm(ks[7], (D_MODEL,), 1.0),
        'w_mod': nrm(ks[8], (DEPTH, D_MODEL, 6 * D_MODEL), D_MODEL ** -0.5),
        'b_mod': nrm(ks[9], (DEPTH, 6 * D_MODEL), 0.01),
        'w_in': nrm(ks[10], (DEPTH, D_MODEL, IN_COLS), D_MODEL ** -0.5),
        'ssd_conv_w': nrm(ks[11], (DEPTH, SSD_CONV, SSD_CONV_CH), SSD_CONV ** -0.5),
        'ssd_conv_b': nrm(ks[13], (DEPTH, SSD_CONV_CH), 0.01),
        'ssd_dt_bias': dt0 + jnp.log(-jnp.expm1(-dt0)),
        'ssd_a_log': jnp.log(jax.random.uniform(ks[14], (DEPTH, 2, SSD_HEADS), jnp.float32, 1.0, 16.0)),
        'ssd_d': 1.0 + nrm(ks[15], (DEPTH, SSD_HEADS), 0.01),
        'ssd_norm_w': 1.0 + nrm(ks[16], (DEPTH, SSD_INNER), 0.01),
        'sc_conv_w': nrm(ks[17], (DEPTH, SC_CONV, SC_WIDTH), SC_CONV ** -0.5),
        'diff_lambda': nrm(ks[18], (DEPTH, 4, DIFF_HD), 0.1),
        'diff_norm_w': 1.0 + nrm(ks[19], (DEPTH, DIFF_VD), 0.01),
        'w_out': nrm(ks[20], (DEPTH, MIX, D_MODEL), BETA * MIX ** -0.5),
        'ln1_g': 1.0 + nrm(ks[21], (DEPTH, D_MODEL), 0.01),
        'ln1_b': nrm(ks[22], (DEPTH, D_MODEL), 0.01),
        'w_up': nrm(ks[23], (DEPTH, D_MODEL, D_FF), D_MODEL ** -0.5),
        'w_down': nrm(ks[24], (DEPTH, D_FF, D_MODEL), BETA * D_FF ** -0.5),
        'ln2_g': 1.0 + nrm(ks[25], (DEPTH, D_MODEL), 0.01),
        'ln2_b': nrm(ks[26], (DEPTH, D_MODEL), 0.01),
    }


def reference(x_prompt, x_sample, cache_k, cache_v, state_ssm_fwd, state_ssm_bwd, c, c_ctx,
              w_mod, b_mod, w_in, ssd_conv_w, ssd_conv_b, ssd_dt_bias, ssd_a_log, ssd_d,
              ssd_norm_w, sc_conv_w, diff_lambda, diff_norm_w, w_out, ln1_g, ln1_b,
              w_up, w_down, ln2_g, ln2_b):
    rows = x_sample.shape[1] // GRID_W
    theta = axial_angles(rows)
    zero_state = jnp.zeros((x_prompt.shape[0], SSD_HEADS, SSD_HEAD_DIM, SSD_STATE), x_prompt.dtype)
    hp, hs = x_prompt, x_sample
    ks_, vs_, sfs_, sbs_ = [], [], [], []
    for l in range(DEPTH):
        p = {
            'w_in': w_in[l], 'ssd_conv_w': ssd_conv_w[l], 'ssd_conv_b': ssd_conv_b[l],
            'ssd_dt_bias': ssd_dt_bias[l], 'ssd_a_log': ssd_a_log[l], 'ssd_d': ssd_d[l],
            'ssd_norm_w': ssd_norm_w[l], 'sc_conv_w': sc_conv_w[l], 'diff_lambda': diff_lambda[l],
            'diff_norm_w': diff_norm_w[l], 'w_out': w_out[l], 'ln1_g': ln1_g[l], 'ln1_b': ln1_b[l],
            'w_up': w_up[l], 'w_down': w_down[l], 'ln2_g': ln2_g[l], 'ln2_b': ln2_b[l],
        }
        lam_init = 0.8 - 0.6 * math.exp(-0.3 * l)
        mod_ctx = (jax.nn.silu(c_ctx) @ w_mod[l] + b_mod[l])[None, None, :]
        hp, k_l, v_l, sf_l, sb_l = trunk_layer(hp, mod_ctx, p, lam_init, None, None, None, zero_state, zero_state)
        ks_.append(k_l)
        vs_.append(v_l)
        sfs_.append(sf_l)
        sbs_.append(sb_l)
        mod_lat = (jax.nn.silu(c) @ w_mod[l] + b_mod[l])[:, None, :]
        hs = trunk_layer(hs, mod_lat, p, lam_init, theta, cache_k[:, l], cache_v[:, l],
                         state_ssm_fwd[:, l], state_ssm_bwd[:, l])[0]
    y_prompt = hp
    y_sample = hs
    new_cache_k = jnp.stack(ks_, axis=1)
    new_cache_v = jnp.stack(vs_, axis=1)
    new_state_ssm_fwd = jnp.stack(sfs_, axis=1)
    new_state_ssm_bwd = jnp.stack(sbs_, axis=1)
    return (y_prompt, y_sample, new_cache_k, new_cache_v, new_state_ssm_fwd, new_state_ssm_bwd)
```

```python
import functools
import math

import jax
import jax.numpy as jnp
import numpy as np
from jax import lax
from jax.experimental import pallas as pl
from jax.experimental.pallas import tpu as pltpu

F32 = jnp.float32
BF16 = jnp.bfloat16

D_MODEL = 1024
DEPTH = 4
GRID_W = 64
SSD_INNER = 512
SSD_HEAD_DIM = 64
SSD_HEADS = 8
SSD_STATE = 128
SSD_CHUNK = 128
SSD_CONV = 5
SSD_CONV_CH = 1024
SC_WIDTH = 256
SC_CONV = 3
DIFF_WIDTH = 256
DIFF_HEADS = 4
DIFF_HD = 32
DIFF_VD = 64
ROPE_BASE = 10000.0
ROPE_F = 8
D_FF = 4096
ALPHA = (2 * DEPTH) ** 0.25
LN_EPS = 1e-5
RMS_EPS = 1e-6

C_Z = 0
C_XBC = 512
C_SCB = 1536
C_SCC = 1792
C_Q = 2304
C_K = 2560
C_V = 2816
C_DT = 3072
IN_COLS_PAD = 3200

VMEM_LIMIT = 56 * 1024 * 1024
NEG_BIG = -1e30

_NT = (((1,), (1,)), ((), ()))
_TN = (((0,), (0,)), ((), ()))


def _split2(x):
    hi = x.astype(BF16)
    lo = (x - hi.astype(F32)).astype(BF16)
    return hi, lo


def _split3(x):
    hi = x.astype(BF16)
    r1 = x - hi.astype(F32)
    mid = r1.astype(BF16)
    lo = (r1 - mid.astype(F32)).astype(BF16)
    return hi, mid, lo


def _silu(x):
    return x * jax.nn.sigmoid(x)


def _layer_norm(x, g, b):
    mu = jnp.mean(x, axis=-1, keepdims=True)
    xc = x - mu
    var = jnp.mean(xc * xc, axis=-1, keepdims=True)
    return xc * lax.rsqrt(var + LN_EPS) * g + b


def _mod_kernel(c_ref, w_ref, b_ref, o_ref):
    a = _silu(c_ref[...]).astype(BF16)
    o_ref[0] = jnp.dot(a, w_ref[0].astype(BF16), preferred_element_type=F32) + b_ref[0]


def _mod_call(cvec, w_mod, b_mod):
    tn = 1536
    return pl.pallas_call(
        _mod_kernel,
        grid=(DEPTH, 6 * D_MODEL // tn),
        in_specs=[
            pl.BlockSpec((16, D_MODEL), lambda l, j: (0, 0)),
            pl.BlockSpec((1, D_MODEL, tn), lambda l, j: (l, 0, j)),
            pl.BlockSpec((1, 1, tn), lambda l, j: (l, 0, j)),
        ],
        out_specs=pl.BlockSpec((1, 16, tn), lambda l, j: (l, 0, j)),
        out_shape=jax.ShapeDtypeStruct((DEPTH, 16, 6 * D_MODEL), F32),
        compiler_params=pltpu.CompilerParams(dimension_semantics=("arbitrary", "arbitrary")),
        name="mod_vectors",
    )(cvec, w_mod, b_mod.reshape(DEPTH, 1, 6 * D_MODEL))


def _inproj_kernel(*refs, tm, n_tiles, use_rope):
    if use_rope:
        (x_ref, xp_ref, xn_ref, mod_ref, w_ref, cw_ref, cb_ref, scw_ref, cos_ref, sin_ref,
         z_ref, xs_ref, b_ref, c_ref, dt_ref, ysc_ref, q_ref, k_ref, v_ref, xbc_scr, sc_scr) = refs
    else:
        (x_ref, xp_ref, xn_ref, mod_ref, w_ref, cw_ref, cb_ref, scw_ref,
         z_ref, xs_ref, b_ref, c_ref, dt_ref, ysc_ref, q_ref, k_ref, v_ref, xbc_scr, sc_scr) = refs
    i = pl.program_id(1)
    sh1 = mod_ref[0, 0:1, :]
    one_sc1 = 1.0 + mod_ref[0, 1:2, :]
    u = (x_ref[0] * one_sc1 + sh1).astype(BF16)
    xh = jnp.concatenate([xp_ref[0], xn_ref[0]], axis=0)
    uh = (xh * one_sc1 + sh1).astype(BF16)
    rowid = lax.broadcasted_iota(jnp.int32, (16, 1), 0)
    valid = jnp.where(rowid < 8, jnp.where(i > 0, 1.0, 0.0), jnp.where(i < n_tiles - 1, 1.0, 0.0))

    def mm(a, lo, hi):
        return jnp.dot(a, w_ref[:, lo:hi], preferred_element_type=F32)

    z_ref[0] = mm(u, C_Z, C_XBC)

    xbc_h = mm(uh, C_XBC, C_SCB) * valid
    xbc_scr[0:8, :] = xbc_h[0:8]
    xbc_scr[8:8 + tm, :] = mm(u, C_XBC, C_SCB)
    xbc_scr[8 + tm:16 + tm, :] = xbc_h[8:16]
    acc = cw_ref[0:1, :] * xbc_scr[pl.ds(6, tm), :] + cb_ref[...]
    for kk in range(1, SSD_CONV):
        acc = acc + cw_ref[kk:kk + 1, :] * xbc_scr[pl.ds(6 + kk, tm), :]
    act = _silu(acc)
    xs_ref[0] = act[:, :SSD_INNER]
    b_ref[0] = act[:, SSD_INNER:SSD_INNER + 256].astype(b_ref.dtype)
    c_ref[0] = act[:, SSD_INNER + 256:].astype(c_ref.dtype)
    dt_ref[0] = mm(u, C_DT, IN_COLS_PAD)

    scb = mm(u, C_SCB, C_SCC)
    ch_m = mm(u, C_SCC, C_Q)
    ch_h = mm(uh, C_SCC, C_Q)
    prod_h = ch_h[:, :SC_WIDTH] * ch_h[:, SC_WIDTH:] * valid
    sc_scr[0:8, :] = prod_h[0:8]
    sc_scr[8:8 + tm, :] = ch_m[:, :SC_WIDTH] * ch_m[:, SC_WIDTH:]
    sc_scr[8 + tm:16 + tm, :] = prod_h[8:16]
    conv = scw_ref[0:1, :] * sc_scr[pl.ds(7, tm), :]
    for kk in range(1, SC_CONV):
        conv = conv + scw_ref[kk:kk + 1, :] * sc_scr[pl.ds(7 + kk, tm), :]
    ysc_ref[0] = scb * conv

    q = mm(u, C_Q, C_K)
    k = mm(u, C_K, C_V)
    v = mm(u, C_V, C_DT)
    if use_rope:
        cos = cos_ref[...]
        sin = sin_ref[...]
        lane = lax.broadcasted_iota(jnp.int32, (tm, DIFF_WIDTH), 1)
        first = (lane & 15) < 8

        def rope(t):
            sw = jnp.where(first, pltpu.roll(t, DIFF_WIDTH - 8, 1), pltpu.roll(t, 8, 1))
            return t * cos + sw * sin

        q = rope(q)
        k = rope(k)
    q_ref[0] = q.astype(q_ref.dtype)
    k_ref[0] = k.astype(k_ref.dtype)
    v_ref[0] = v.astype(v_ref.dtype)


def _inproj_call(x, mod, w_in_p, conv_w, conv_b, sc_w, rope, kv_dtype, tm=256):
    bsz, n, _ = x.shape
    n_tiles = n // tm
    tpb = tm // 8
    bm = mod.shape[0]
    mod_idx = (lambda b, i: (b, 0, 0)) if bm > 1 else (lambda b, i: (0, 0, 0))
    tok = lambda w: pl.BlockSpec((1, tm, w), lambda b, i: (b, i, 0))
    const = lambda shp: pl.BlockSpec(shp, lambda b, i: (0,) * len(shp))
    in_specs = [
        tok(D_MODEL),
        pl.BlockSpec((1, 8, D_MODEL), lambda b, i: (b, jnp.maximum(i * tpb - 1, 0), 0)),
        pl.BlockSpec((1, 8, D_MODEL), lambda b, i: (b, jnp.minimum((i + 1) * tpb, n // 8 - 1), 0)),
        pl.BlockSpec((1, 6, D_MODEL), mod_idx),
        const((D_MODEL, IN_COLS_PAD)),
        const((SSD_CONV, SSD_CONV_CH)),
        const((1, SSD_CONV_CH)),
        const((SC_CONV, SC_WIDTH)),
    ]
    args = [x, x, x, mod, w_in_p, conv_w, conv_b, sc_w]
    use_rope = rope is not None
    if use_rope:
        in_specs += [pl.BlockSpec((tm, DIFF_WIDTH), lambda b, i: (i, 0))] * 2
        args += list(rope)
    sds = lambda w, dt: jax.ShapeDtypeStruct((bsz, n, w), dt)
    out_shape = (sds(512, F32), sds(512, F32), sds(256, BF16), sds(256, BF16), sds(128, F32),
                 sds(256, F32), sds(256, BF16), sds(256, kv_dtype), sds(256, kv_dtype))
    out_specs = tuple(tok(s.shape[-1]) for s in out_shape)
    return pl.pallas_call(
        functools.partial(_inproj_kernel, tm=tm, n_tiles=n_tiles, use_rope=use_rope),
        grid=(bsz, n_tiles),
        in_specs=in_specs,
        out_specs=out_specs,
        out_shape=out_shape,
        scratch_shapes=[pltpu.VMEM((tm + 16, SSD_CONV_CH), F32), pltpu.VMEM((tm + 16, SC_WIDTH), F32)],
        compiler_params=pltpu.CompilerParams(dimension_semantics=("arbitrary", "arbitrary"),
                                             vmem_limit_bytes=VMEM_LIMIT),
        name="in_proj_rope" if use_rope else "in_proj",
    )(*args)


def _expand_matrix():
    e = np.zeros((256, 3 * 1024), np.float32)
    for part in range(2):
        for qn in range(3):
            for d in range(2):
                for h in range(SSD_HEADS):
                    r = part * 128 + qn * 16 + d * 8 + h
                    c0 = qn * 1024 + d * 512 + h * 64
                    e[r, c0:c0 + 64] = 1.0
    return jnp.asarray(e, dtype=BF16)


def _ssd_kernel(xs_ref, b_ref, c_ref, dt_ref, z_ref, s0f_ref, s0b_ref, prm_ref, dexp_ref, nw_ref, emat_ref,
                y_ref, sf_ref, sb_ref, st_scr, gxb_scr, tb_scr, cdb_scr, *, nc):
    lane = lax.broadcasted_iota(jnp.int32, (1, 128), 1)
    ii = lax.broadcasted_iota(jnp.int32, (128, 128), 0)
    jj = lax.broadcasted_iota(jnp.int32, (128, 128), 1)
    low = jj <= ii
    upp = jj >= ii
    ltri = jnp.where(low, 1.0, 0.0).astype(BF16)
    bias = prm_ref[0:1, :]
    a_row = -jnp.exp(prm_ref[1:2, :])
    is_head = lane < 2 * SSD_HEADS
    is_fwd = lane < SSD_HEADS
    half = lane < 64

    st_scr[0] = s0f_ref[0]
    st_scr[1] = s0b_ref[0]

    def fwd_body(c, carry):
        r0 = pl.multiple_of(c * SSD_CHUNK, SSD_CHUNK)
        rows = pl.ds(r0, SSD_CHUNK)
        dtv = jnp.where(is_head, jax.nn.softplus(dt_ref[0, rows, :] + bias), 0.0)
        dta = dtv * a_row
        a3 = jnp.dot(ltri, jnp.concatenate(_split3(dta), axis=1), preferred_element_type=F32)
        cum = a3[:, 0:128] + a3[:, 128:256] + a3[:, 256:384]
        total = cum[127:128, :]
        acum = jnp.where(is_fwd, cum, total - cum + dta)
        gdec = jnp.where(is_head, jnp.exp(acum), 0.0)
        wdec = dtv * jnp.exp(total - acum)
        packed = dtv + pltpu.roll(wdec, 16, 1) + pltpu.roll(gdec, 32, 1)
        x6 = jnp.dot(jnp.concatenate(_split2(packed), axis=1), emat_ref[...], preferred_element_type=F32)
        dtx_f, dtx_b = x6[:, 0:512], x6[:, 512:1024]
        wx_f, wx_b = x6[:, 1024:1536], x6[:, 1536:2048]
        gx_f, gx_b = x6[:, 2048:2560], x6[:, 2560:3072]
        acum_t = acum.T

        x = xs_ref[0, rows, :]
        xdt_f = (x * dtx_f).astype(BF16)
        xdt_b = (x * dtx_b).astype(BF16)
        xd_f = (x * wx_f).astype(BF16)
        xd_b = (x * wx_b).astype(BF16)
        bc = b_ref[0, rows, :]
        cc = c_ref[0, rows, :]
        gxb_scr[rows, :] = gx_b

        y_parts = []
        for g in range(2):
            gl = slice(g * 256, (g + 1) * 256)
            bg = bc[:, g * 128:(g + 1) * 128]
            cg = cc[:, g * 128:(g + 1) * 128]
            cb = lax.dot_general(cg, bg, _NT, preferred_element_type=F32)
            s_f = st_scr[0, :, gl]
            yo_f = jnp.dot(cg, s_f.astype(BF16), preferred_element_type=F32) * gx_f[:, gl]
            st_scr[0, :, gl] = s_f * gx_f[127:128, gl] + lax.dot_general(
                bg, xd_f[:, gl], _TN, preferred_element_type=F32)
            tb_scr[c, :, gl] = lax.dot_general(bg, xd_b[:, gl], _TN, preferred_element_type=F32)
            cdb_scr[c, 0:1, gl] = gx_b[0:1, gl]
            for pair in range(2):
                hp = 2 * g + pair
                pl_ = slice(hp * 128, (hp + 1) * 128)
                rhs = jnp.concatenate([xdt_f[:, pl_], xdt_b[:, pl_]], axis=0)
                ys = []
                for sub in range(2):
                    h = 2 * hp + sub
                    hb = SSD_HEADS + h
                    seg_f = acum[:, h:h + 1] - acum_t[h:h + 1, :]
                    m_f = jnp.exp(jnp.where(low, seg_f, NEG_BIG)) * cb
                    seg_b = acum[:, hb:hb + 1] - acum_t[hb:hb + 1, :]
                    m_b = jnp.exp(jnp.where(upp, seg_b, NEG_BIG)) * cb
                    lhs = jnp.concatenate([m_f, m_b], axis=1).astype(BF16)
                    ys.append(jnp.dot(lhs, rhs, preferred_element_type=F32))
                y_pair = jnp.where(half, ys[0], ys[1])
                y_parts.append(y_pair + yo_f[:, pair * 128:(pair + 1) * 128])
        y_ref[0, rows, :] = jnp.concatenate(y_parts, axis=1)
        return carry

    lax.fori_loop(0, nc, fwd_body, 0)
    sf_ref[0] = st_scr[0]

    dexp = dexp_ref[...]
    nw = nw_ref[...]

    def bwd_body(t, carry):
        c = nc - 1 - t
        r0 = pl.multiple_of(c * SSD_CHUNK, SSD_CHUNK)
        rows = pl.ds(r0, SSD_CHUNK)
        cc = c_ref[0, rows, :]
        gx_b = gxb_scr[rows, :]
        yo = []
        for g in range(2):
            gl = slice(g * 256, (g + 1) * 256)
            cg = cc[:, g * 128:(g + 1) * 128]
            s_b = st_scr[1, :, gl]
            yo.append(jnp.dot(cg, s_b.astype(BF16), preferred_element_type=F32) * gx_b[:, gl])
            st_scr[1, :, gl] = s_b * cdb_scr[c, 0:1, gl] + tb_scr[c, :, gl]
        x = xs_ref[0, rows, :]
        y = y_ref[0, rows, :] + jnp.concatenate(yo, axis=1) + dexp * x
        y = y * _silu(z_ref[0, rows, :])
        ms = jnp.mean(y * y, axis=-1, keepdims=True)
        y_ref[0, rows, :] = y * lax.rsqrt(ms + RMS_EPS) * nw
        return carry

    lax.fori_loop(0, nc, bwd_body, 0)
    sb_ref[0] = st_scr[1]


def _ssd_call(xs, bm, cm, dt, z, s0f, s0b, prm, dexp, nw, emat):
    bsz, n, _ = xs.shape
    nc = n // SSD_CHUNK
    seq = lambda w: pl.BlockSpec((1, n, w), lambda b: (b, 0, 0))
    st = pl.BlockSpec((1, SSD_STATE, SSD_INNER), lambda b: (b, 0, 0))
    const = lambda shp: pl.BlockSpec(shp, lambda b: (0,) * len(shp))
    return pl.pallas_call(
        functools.partial(_ssd_kernel, nc=nc),
        grid=(bsz,),
        in_specs=[seq(512), seq(256), seq(256), seq(128), seq(512), st, st,
                  const((8, 128)), const((1, 512)), const((1, 512)), const((256, 3072))],
        out_specs=(seq(512), st, st),
        out_shape=(jax.ShapeDtypeStruct((bsz, n, SSD_INNER), F32),
                   jax.ShapeDtypeStruct((bsz, SSD_STATE, SSD_INNER), F32),
                   jax.ShapeDtypeStruct((bsz, SSD_STATE, SSD_INNER), F32)),
        scratch_shapes=[pltpu.VMEM((2, SSD_STATE, SSD_INNER), F32),
                        pltpu.VMEM((n, SSD_INNER), F32),
                        pltpu.VMEM((nc, SSD_STATE, SSD_INNER), F32),
                        pltpu.VMEM((nc, 8, SSD_INNER), F32)],
        compiler_params=pltpu.CompilerParams(dimension_semantics=("arbitrary",),
                                             vmem_limit_bytes=VMEM_LIMIT),
        name="ssd_scan",
    )(xs, bm, cm, dt, z, s0f, s0b, prm, dexp, nw, emat)


def _attn_kernel(q_ref, k_ref, v_ref, lp_ref, nw_ref, ones_ref, o_ref, *, lam_init):
    q = q_ref[0].astype(BF16)
    k = k_ref[0].astype(BF16)
    v = v_ref[0].astype(BF16)
    lp = lp_ref[...]
    lam = (jnp.exp(jnp.sum(lp[0:1] * lp[1:2], axis=-1, keepdims=True))
           - jnp.exp(jnp.sum(lp[2:3] * lp[3:4], axis=-1, keepdims=True)) + lam_init)
    scale = DIFF_HD ** -0.5
    lane = lax.broadcasted_iota(jnp.int32, (1, 128), 1)
    outs = []
    for g in range(2):
        qg = q[:, g * 128:(g + 1) * 128]
        kg = k[:, g * 128:(g + 1) * 128]
        vg = v[:, g * 128:(g + 1) * 128]
        o_heads = []
        for hh in range(2):
            probs = []
            for s in range(2):
                lo = (2 * hh + s) * DIFF_HD
                qm = jnp.where((lane >= lo) & (lane < lo + DIFF_HD), qg, jnp.zeros_like(qg))
                sc = lax.dot_general(qm, kg, _NT, preferred_element_type=F32)
                mx = jnp.max(sc, axis=-1, keepdims=True)
                p = jnp.exp((sc - mx) * scale)
                probs.append((p, jnp.sum(p, axis=-1, keepdims=True)))
            pd = probs[0][0] * (1.0 / probs[0][1]) - probs[1][0] * (lam / probs[1][1])
            o_heads.append(jnp.dot(pd.astype(BF16), vg, preferred_element_type=F32))
        outs.append(jnp.where(lane < DIFF_VD, o_heads[0], o_heads[1]))
    o = jnp.concatenate(outs, axis=1)
    ms = jnp.dot(jnp.concatenate(_split2(o * o), axis=1), ones_ref[...],
                 preferred_element_type=F32) * (1.0 / DIFF_VD)
    o_ref[0] = o * lax.rsqrt(ms + RMS_EPS) * nw_ref[...] * (1.0 - lam_init)


def _head_ones():
    e = np.zeros((512, 256), np.float32)
    for part in range(2):
        for h in range(DIFF_HEADS):
            e[part * 256 + h * 64:part * 256 + (h + 1) * 64, h * 64:(h + 1) * 64] = 1.0
    return jnp.asarray(e, dtype=BF16)


def _attn_call(q, k, v, lp, nw, ones, lam_init, tq=256):
    bsz, n, _ = q.shape
    m = k.shape[1]
    const = lambda shp: pl.BlockSpec(shp, lambda b, i: (0,) * len(shp))
    return pl.pallas_call(
        functools.partial(_attn_kernel, lam_init=lam_init),
        grid=(bsz, n // tq),
        in_specs=[pl.BlockSpec((1, tq, 256), lambda b, i: (b, i, 0)),
                  pl.BlockSpec((1, m, 256), lambda b, i: (b, 0, 0)),
                  pl.BlockSpec((1, m, 256), lambda b, i: (b, 0, 0)),
                  const((4, DIFF_HD)), const((1, 256)), const((512, 256))],
        out_specs=pl.BlockSpec((1, tq, 256), lambda b, i: (b, i, 0)),
        out_shape=jax.ShapeDtypeStruct((bsz, n, 256), F32),
        compiler_params=pltpu.CompilerParams(dimension_semantics=("arbitrary", "arbitrary"),
                                             vmem_limit_bytes=VMEM_LIMIT),
        name="diff_attn",
    )(q, k, v, lp, nw, ones)


def _mlp_kernel(x_ref, yssd_ref, ysc_ref, yat_ref, mod_ref, wo_ref, wu_ref, wd_ref, ln_ref, o_ref):
    x = x_ref[0]
    m = (jnp.dot(yssd_ref[0].astype(BF16), wo_ref[0:512, :], preferred_element_type=F32)
         + jnp.dot(ysc_ref[0].astype(BF16), wo_ref[512:768, :], preferred_element_type=F32)
         + jnp.dot(yat_ref[0].astype(BF16), wo_ref[768:1024, :], preferred_element_type=F32))
    g1 = mod_ref[0, 2:3, :]
    sh2 = mod_ref[0, 3:4, :]
    sc2 = mod_ref[0, 4:5, :]
    g2 = mod_ref[0, 5:6, :]
    x1 = _layer_norm(ALPHA * x + g1 * m, ln_ref[0:1, :], ln_ref[1:2, :])
    h = (x1 * (1.0 + sc2) + sh2).astype(BF16)
    up = jnp.dot(h, wu_ref[...], preferred_element_type=F32)
    act = jnp.square(jnp.maximum(up, 0.0)).astype(BF16)
    f = jnp.dot(act, wd_ref[...], preferred_element_type=F32)
    o_ref[0] = _layer_norm(ALPHA * x1 + g2 * f, ln_ref[2:3, :], ln_ref[3:4, :])


def _mlp_call(x, yssd, ysc, yat, mod, wo, wu, wd, ln, tm=256):
    bsz, n, _ = x.shape
    bm = mod.shape[0]
    mod_idx = (lambda b, i: (b, 0, 0)) if bm > 1 else (lambda b, i: (0, 0, 0))
    tok = lambda w: pl.BlockSpec((1, tm, w), lambda b, i: (b, i, 0))
    const = lambda shp: pl.BlockSpec(shp, lambda b, i: (0,) * len(shp), pipeline_mode=pl.Buffered(1))
    return pl.pallas_call(
        _mlp_kernel,
        grid=(bsz, n // tm),
        in_specs=[tok(D_MODEL), tok(512), tok(256), tok(256),
                  pl.BlockSpec((1, 6, D_MODEL), mod_idx),
                  const((D_MODEL, D_MODEL)), const((D_MODEL, D_FF)), const((D_FF, D_MODEL)),
                  const((4, D_MODEL))],
        out_specs=tok(D_MODEL),
        out_shape=jax.ShapeDtypeStruct((bsz, n, D_MODEL), F32),
        compiler_params=pltpu.CompilerParams(dimension_semantics=("arbitrary", "arbitrary"),
                                             vmem_limit_bytes=VMEM_LIMIT),
        name="out_mlp",
    )(x, yssd, ysc, yat, mod, wo, wu, wd, ln)


def _rope_tables(n):
    rows = n // GRID_W
    row = jnp.repeat(jnp.arange(rows, dtype=F32), GRID_W)
    col = jnp.tile(jnp.arange(GRID_W, dtype=F32), rows)
    inv = ROPE_BASE ** (-jnp.arange(ROPE_F, dtype=F32) / ROPE_F)
    ang = jnp.stack([row[:, None] * inv, col[:, None] * inv], axis=1)
    ang = jnp.repeat(ang[:, :, None, :], 2, axis=2).reshape(n, DIFF_HD)
    sign = jnp.tile(jnp.concatenate([-jnp.ones(ROPE_F, F32), jnp.ones(ROPE_F, F32)]), 2)
    cos_t = jnp.tile(jnp.cos(ang), (1, 2 * DIFF_HEADS))
    sin_t = jnp.tile(jnp.sin(ang) * sign, (1, 2 * DIFF_HEADS))
    return cos_t, sin_t


def _state_to_kernel(s):
    b = s.shape[0]
    return s.transpose(0, 3, 1, 2).reshape(b, SSD_STATE, SSD_INNER)


def _state_from_kernel(s):
    b = s.shape[0]
    return s.reshape(b, SSD_STATE, SSD_HEADS, SSD_HEAD_DIM).transpose(0, 2, 3, 1)


def _trunk_layer(x, mod, lw, lam_init, rope, prefix, s0):
    bsz, n, _ = x.shape
    kv_dtype = F32 if prefix is None else BF16
    z, xs, bm, cm, dt, ysc, q, k, v = _inproj_call(
        x, mod, lw["w_in"], lw["conv_w"], lw["conv_b"], lw["sc_w"], rope, kv_dtype)
    yssd, s_f, s_b = _ssd_call(xs, bm, cm, dt, z, s0[0], s0[1], lw["prm"], lw["dexp"], lw["ssd_nw"], lw["emat"])
    if prefix is None:
        keys, vals = k, v
    else:
        keys = jnp.concatenate([prefix[0], k], axis=1)
        vals = jnp.concatenate([prefix[1], v], axis=1)
    yat = _attn_call(q, keys, vals, lw["lp"], lw["attn_nw"], lw["ones"], lam_init)
    y = _mlp_call(x, yssd, ysc, yat, mod, lw["w_out"], lw["w_up"], lw["w_down"], lw["ln"])
    return y, k, v, s_f, s_b


def kernel(x_prompt, x_sample, cache_k, cache_v, state_ssm_fwd, state_ssm_bwd, c, c_ctx, w_mod, b_mod, w_in,
           ssd_conv_w, ssd_conv_b, ssd_dt_bias, ssd_a_log, ssd_d, ssd_norm_w, sc_conv_w, diff_lambda,
           diff_norm_w, w_out, ln1_g, ln1_b, w_up, w_down, ln2_g, ln2_b):
    bp, seq, _ = x_prompt.shape
    bs, dec_seq, _ = x_sample.shape
    past = cache_k.shape[2]

    cvec = jnp.zeros((16, D_MODEL), F32).at[:bs].set(c).at[bs].set(c_ctx)
    mods = _mod_call(cvec, w_mod, b_mod)
    rope = _rope_tables(dec_seq)
    emat = _expand_matrix()
    ones = _head_ones()
    zero_state = jnp.zeros((bp, SSD_STATE, SSD_INNER), F32)

    hp, hs = x_prompt, x_sample
    ks_, vs_, sfs_, sbs_ = [], [], [], []
    for l in range(DEPTH):
        lam_init = 0.8 - 0.6 * math.exp(-0.3 * l)
        wl = w_in[l]
        w_in_p = jnp.concatenate(
            [wl[:, :1536], wl[:, 1552:], wl[:, 1536:1552], jnp.zeros((D_MODEL, IN_COLS_PAD - 3088), F32)],
            axis=1).astype(BF16)
        prm = jnp.zeros((8, 128), F32)
        prm = prm.at[0, :16].set(ssd_dt_bias[l].reshape(16)).at[1, :16].set(ssd_a_log[l].reshape(16))
        lw = {
            "w_in": w_in_p,
            "conv_w": ssd_conv_w[l],
            "conv_b": ssd_conv_b[l].reshape(1, SSD_CONV_CH),
            "sc_w": sc_conv_w[l],
            "prm": prm,
            "dexp": jnp.repeat(ssd_d[l], SSD_HEAD_DIM).reshape(1, SSD_INNER),
            "ssd_nw": ssd_norm_w[l].reshape(1, SSD_INNER),
            "emat": emat,
            "lp": diff_lambda[l],
            "attn_nw": jnp.tile(diff_norm_w[l], DIFF_HEADS).reshape(1, DIFF_WIDTH),
            "ones": ones,
            "w_out": w_out[l].astype(BF16),
            "w_up": w_up[l].astype(BF16),
            "w_down": w_down[l].astype(BF16),
            "ln": jnp.stack([ln1_g[l], ln1_b[l], ln2_g[l], ln2_b[l]], axis=0),
        }
        mod_ctx = mods[l, bs:bs + 1].reshape(1, 6, D_MODEL)
        mod_lat = mods[l, :bs].reshape(bs, 6, D_MODEL)
        hp, k_l, v_l, sf_l, sb_l = _trunk_layer(hp, mod_ctx, lw, lam_init, None, None, (zero_state, zero_state))
        ks_.append(k_l)
        vs_.append(v_l)
        sfs_.append(_state_from_kernel(sf_l))
        sbs_.append(_state_from_kernel(sb_l))
        prefix = (cache_k[:, l].reshape(bs, past, DIFF_WIDTH).astype(BF16),
                  cache_v[:, l].reshape(bs, past, DIFF_WIDTH).astype(BF16))
        s0 = (_state_to_kernel(state_ssm_fwd[:, l]), _state_to_kernel(state_ssm_bwd[:, l]))
        hs = _trunk_layer(hs, mod_lat, lw, lam_init, rope, prefix, s0)[0]

    new_cache_k = jnp.stack(ks_, axis=1).reshape(bp, DEPTH, seq, 2 * DIFF_HEADS, DIFF_HD)
    new_cache_v = jnp.stack(vs_, axis=1).reshape(bp, DEPTH, seq, DIFF_HEADS, DIFF_VD)
    return (hp, hs, new_cache_k, new_cache_v, jnp.stack(sfs_, axis=1), jnp.stack(sbs_, axis=1))
```

```python
import functools
import math

import jax
import jax.numpy as jnp
import numpy as np
from jax import lax
from jax.experimental import pallas as pl
from jax.experimental.pallas import tpu as pltpu

F32 = jnp.float32
BF16 = jnp.bfloat16

D_MODEL = 1024
DEPTH = 4
GRID_W = 64
SSD_INNER = 512
SSD_HEAD_DIM = 64
SSD_HEADS = 8
SSD_STATE = 128
SSD_CHUNK = 128
SSD_CONV = 5
SSD_CONV_CH = 1024
SC_WIDTH = 256
SC_CONV = 3
DIFF_WIDTH = 256
DIFF_HEADS = 4
DIFF_HD = 32
DIFF_VD = 64
ROPE_BASE = 10000.0
ROPE_F = 8
D_FF = 4096
ALPHA = (2 * DEPTH) ** 0.25
LN_EPS = 1e-5
RMS_EPS = 1e-6

C_Z = 0
C_XBC = 512
C_SCB = 1536
C_SCC = 1792
C_Q = 2304
C_K = 2560
C_V = 2816
C_DT = 3072
IN_COLS = 3088
IN_COLS_PAD = 3200

HALO = 8
VMEM_LIMIT = 56 * 1024 * 1024
NEG_BIG = -1e30
Q_PRESCALE = DIFF_HD ** -0.5 * math.log2(math.e)

_NT = (((1,), (1,)), ((), ()))
_TN = (((0,), (0,)), ((), ()))


def _split2(x):
    hi = x.astype(BF16)
    lo = (x - hi.astype(F32)).astype(BF16)
    return hi, lo


def _split3(x):
    hi = x.astype(BF16)
    r1 = x - hi.astype(F32)
    mid = r1.astype(BF16)
    lo = (r1 - mid.astype(F32)).astype(BF16)
    return hi, mid, lo


def _silu(x):
    return x * jax.nn.sigmoid(x)


def _layer_norm(x, g, b):
    mu = jnp.mean(x, axis=-1, keepdims=True)
    xc = x - mu
    var = jnp.mean(xc * xc, axis=-1, keepdims=True)
    return xc * lax.rsqrt(var + LN_EPS) * g + b


def _params(*sem):
    return pltpu.CompilerParams(dimension_semantics=sem, vmem_limit_bytes=VMEM_LIMIT)


def _mod_kernel(c_ref, w_ref, b_ref, o_ref):
    a = _silu(c_ref[...]).astype(BF16)
    o_ref[0] = jnp.dot(a, w_ref[0].astype(BF16), preferred_element_type=F32) + b_ref[0]


def _mod_call(cvec, w_mod, b_mod):
    tn = 1536
    return pl.pallas_call(
        _mod_kernel,
        grid=(DEPTH, 6 * D_MODEL // tn),
        in_specs=[
            pl.BlockSpec((16, D_MODEL), lambda l, j: (0, 0)),
            pl.BlockSpec((1, D_MODEL, tn), lambda l, j: (l, 0, j)),
            pl.BlockSpec((1, 1, tn), lambda l, j: (l, 0, j)),
        ],
        out_specs=pl.BlockSpec((1, 16, tn), lambda l, j: (l, 0, j)),
        out_shape=jax.ShapeDtypeStruct((DEPTH, 16, 6 * D_MODEL), F32),
        compiler_params=_params("arbitrary", "arbitrary"),
        name="mod_vectors",
    )(cvec, w_mod, b_mod.reshape(DEPTH, 1, 6 * D_MODEL))


def _inproj_kernel(*refs, tm, n_tiles, use_rope, cache_out):
    refs = list(refs)
    x_ref, xp_ref, xn_ref, mod_ref, w_ref, cw_ref, cb_ref, scw_ref = refs[:8]
    pos = 8
    if use_rope:
        cos_ref, sin_ref = refs[pos:pos + 2]
        pos += 2
    if cache_out:
        pos += 2
    else:
        wvt_ref = refs[pos]
        pos += 1
    z_ref, xs_ref, b_ref, c_ref, dt_ref, ysc_ref, q_ref, k_ref, v_ref = refs[pos:pos + 9]
    i = pl.program_id(1)
    sh1 = mod_ref[0, 0:1, :]
    one_sc1 = 1.0 + mod_ref[0, 1:2, :]
    rows_e = tm + 2 * HALO
    xe = jnp.concatenate([xp_ref[0], x_ref[0], xn_ref[0]], axis=0)
    ue = (xe * one_sc1 + sh1).astype(BF16)
    u = (x_ref[0] * one_sc1 + sh1).astype(BF16)
    rowid = lax.broadcasted_iota(jnp.int32, (rows_e, 1), 0)
    keep = jnp.where(rowid < HALO, jnp.where(i > 0, 1.0, 0.0),
                     jnp.where(rowid >= tm + HALO, jnp.where(i < n_tiles - 1, 1.0, 0.0), 1.0))

    def mm(a, lo, hi):
        return jnp.dot(a, w_ref[0, :, lo:hi], preferred_element_type=F32)

    def shifted(t, off):
        r = t if off == 0 else pltpu.roll(t, (rows_e - off) % rows_e, 0)
        return r[HALO:HALO + tm]

    z_ref[0] = mm(u, C_Z, C_XBC)
    dt_ref[0] = mm(u, C_DT, IN_COLS_PAD)

    xbc = mm(ue, C_XBC, C_SCB) * keep
    acc = cw_ref[0, 0:1, :] * shifted(xbc, -2) + cb_ref[0]
    for kk in range(1, SSD_CONV):
        acc = acc + cw_ref[0, kk:kk + 1, :] * shifted(xbc, kk - 2)
    act = _silu(acc)
    xs_ref[0] = act[:, :SSD_INNER]
    b_ref[0] = act[:, SSD_INNER:SSD_INNER + 256].astype(b_ref.dtype)
    c_ref[0] = act[:, SSD_INNER + 256:].astype(c_ref.dtype)

    scb = mm(u, C_SCB, C_SCC)
    ch = mm(ue, C_SCC, C_Q)
    prod = ch[:, :SC_WIDTH] * ch[:, SC_WIDTH:] * keep
    conv = scw_ref[0, 0:1, :] * shifted(prod, -1)
    for kk in range(1, SC_CONV):
        conv = conv + scw_ref[0, kk:kk + 1, :] * shifted(prod, kk - 1)
    ysc_ref[0] = scb * conv

    q = mm(u, C_Q, C_K)
    k = mm(u, C_K, C_V)
    if use_rope:
        cos = cos_ref[...]
        sin = sin_ref[...]
        lane = lax.broadcasted_iota(jnp.int32, (tm, DIFF_WIDTH), 1)
        first = (lane & 15) < 8

        def rope(t):
            sw = jnp.where(first, pltpu.roll(t, DIFF_WIDTH - 8, 1), pltpu.roll(t, 8, 1))
            return t * cos + sw * sin

        q = rope(q)
        k = rope(k)
    q_ref[0] = (q * Q_PRESCALE).astype(q_ref.dtype)
    if cache_out:
        k_ref[0, 0] = k
        v_ref[0, 0] = mm(u, C_V, C_DT)
    else:
        k_ref[0] = k.astype(k_ref.dtype)
        v_ref[0] = lax.dot_general(wvt_ref[0], u, _NT, preferred_element_type=F32).astype(v_ref.dtype)


def _inproj_call(x, mod, lw, l, rope, caches, tm):
    bsz, n, _ = x.shape
    n_tiles = n // tm
    tpb = tm // 8
    bm = mod.shape[0]
    mod_idx = (lambda b, i: (b, 0, 0)) if bm > 1 else (lambda b, i: (0, 0, 0))
    tok = lambda w: pl.BlockSpec((1, tm, w), lambda b, i: (b, i, 0))
    layer = lambda shp: pl.BlockSpec((1,) + shp, lambda b, i: (l,) + (0,) * len(shp))
    in_specs = [
        tok(D_MODEL),
        pl.BlockSpec((1, HALO, D_MODEL), lambda b, i: (b, jnp.maximum(i * tpb - 1, 0), 0)),
        pl.BlockSpec((1, HALO, D_MODEL), lambda b, i: (b, jnp.minimum((i + 1) * tpb, n // 8 - 1), 0)),
        pl.BlockSpec((1, 6, D_MODEL), mod_idx),
        layer((D_MODEL, IN_COLS_PAD)),
        layer((SSD_CONV, SSD_CONV_CH)),
        layer((1, SSD_CONV_CH)),
        layer((SC_CONV, SC_WIDTH)),
    ]
    args = [x, x, x, mod, lw["w_in"], lw["conv_w"], lw["conv_b"], lw["sc_w"]]
    use_rope = rope is not None
    if use_rope:
        in_specs += [pl.BlockSpec((tm, DIFF_WIDTH), lambda b, i: (i, 0))] * 2
        args += list(rope)
    sds = lambda w, dt: jax.ShapeDtypeStruct((bsz, n, w), dt)
    out_shape = [sds(512, F32), sds(512, F32), sds(256, BF16), sds(256, BF16), sds(128, F32),
                 sds(256, F32), sds(256, BF16)]
    out_specs = [tok(s.shape[-1]) for s in out_shape]
    aliases = {}
    cache_out = caches is not None
    if cache_out:
        aliases = {len(args): 7, len(args) + 1: 8}
        in_specs += [pl.BlockSpec(memory_space=pl.ANY)] * 2
        args += list(caches)
        out_shape += [jax.ShapeDtypeStruct(caches[0].shape, F32)] * 2
        out_specs += [pl.BlockSpec((1, 1, tm, DIFF_WIDTH), lambda b, i: (b, l, i, 0))] * 2
    else:
        in_specs += [layer((DIFF_WIDTH, D_MODEL))]
        args += [lw["w_vt"]]
        out_shape += [sds(256, BF16), jax.ShapeDtypeStruct((bsz, DIFF_WIDTH, n), BF16)]
        out_specs += [tok(256), pl.BlockSpec((1, DIFF_WIDTH, tm), lambda b, i: (b, 0, i))]
    return pl.pallas_call(
        functools.partial(_inproj_kernel, tm=tm, n_tiles=n_tiles, use_rope=use_rope, cache_out=cache_out),
        grid=(bsz, n_tiles),
        in_specs=in_specs,
        out_specs=tuple(out_specs),
        out_shape=tuple(out_shape),
        input_output_aliases=aliases,
        compiler_params=_params("arbitrary", "arbitrary"),
        name="in_proj_rope" if use_rope else "in_proj",
    )(*args)


def _expand_matrix():
    e = np.zeros((256, 3 * 1024), np.float32)
    for part in range(2):
        for qn in range(3):
            for d in range(2):
                for h in range(SSD_HEADS):
                    r = part * 128 + qn * 16 + d * 8 + h
                    c0 = qn * 1024 + d * 512 + h * 64
                    e[r, c0:c0 + 64] = 1.0
    return jnp.asarray(e, dtype=BF16)


def _ssd_kernel(*refs, nc, has_init, emit_state):
    refs = list(refs)
    xs_ref, b_ref, c_ref, dt_ref, z_ref, prm_ref, dexp_ref, nw_ref, emat_ref = refs[:9]
    pos = 9
    if has_init:
        s0f_ref, s0b_ref = refs[pos:pos + 2]
        pos += 2
    if emit_state:
        pos += 2
    y_ref = refs[pos]
    pos += 1
    if emit_state:
        sf_ref, sb_ref = refs[pos:pos + 2]
        pos += 2
    st_scr, gxb_scr, tb_scr, cdb_scr = refs[pos:pos + 4]

    lane = lax.broadcasted_iota(jnp.int32, (1, 128), 1)
    ii = lax.broadcasted_iota(jnp.int32, (128, 128), 0)
    jj = lax.broadcasted_iota(jnp.int32, (128, 128), 1)
    low = jj <= ii
    upp = jj >= ii
    ltri = jnp.where(low, 1.0, 0.0).astype(BF16)
    bias = prm_ref[0, 0:1, :]
    a_row = -jnp.exp(prm_ref[0, 1:2, :])
    is_head = lane < 2 * SSD_HEADS
    is_fwd = lane < SSD_HEADS
    half = lane < 64

    for blk in range(4):
        bl = slice(blk * 128, (blk + 1) * 128)
        if has_init:
            st_scr[0, :, bl] = s0f_ref[0, 0, bl, :].T
            st_scr[1, :, bl] = s0b_ref[0, 0, bl, :].T
        else:
            st_scr[0, :, bl] = jnp.zeros((SSD_STATE, 128), F32)
            st_scr[1, :, bl] = jnp.zeros((SSD_STATE, 128), F32)

    def fwd_body(c, carry):
        r0 = pl.multiple_of(c * SSD_CHUNK, SSD_CHUNK)
        rows = pl.ds(r0, SSD_CHUNK)
        dtv = jnp.where(is_head, jax.nn.softplus(dt_ref[0, rows, :] + bias), 0.0)
        dta = dtv * a_row
        a3 = jnp.dot(ltri, jnp.concatenate(_split3(dta), axis=1), preferred_element_type=F32)
        cum = a3[:, 0:128] + a3[:, 128:256] + a3[:, 256:384]
        total = cum[127:128, :]
        acum = jnp.where(is_fwd, cum, total - cum + dta)
        gdec = jnp.where(is_head, jnp.exp(acum), 0.0)
        wdec = dtv * jnp.exp(total - acum)
        packed = dtv + pltpu.roll(wdec, 16, 1) + pltpu.roll(gdec, 32, 1)
        x6 = jnp.dot(jnp.concatenate(_split2(packed), axis=1), emat_ref[...], preferred_element_type=F32)
        dtx_f, dtx_b = x6[:, 0:512], x6[:, 512:1024]
        wx_f, wx_b = x6[:, 1024:1536], x6[:, 1536:2048]
        gx_f, gx_b = x6[:, 2048:2560], x6[:, 2560:3072]
        acum_t = acum.T

        x = xs_ref[0, rows, :]
        xdt_f = (x * dtx_f).astype(BF16)
        xdt_b = (x * dtx_b).astype(BF16)
        xd_f = (x * wx_f).astype(BF16)
        xd_b = (x * wx_b).astype(BF16)
        bc = b_ref[0, rows, :]
        cc = c_ref[0, rows, :]
        gxb_scr[rows, :] = gx_b

        y_parts = []
        for g in range(2):
            gl = slice(g * 256, (g + 1) * 256)
            bg = bc[:, g * 128:(g + 1) * 128]
            cg = cc[:, g * 128:(g + 1) * 128]
            cb = lax.dot_general(cg, bg, _NT, preferred_element_type=F32)
            s_f = st_scr[0, :, gl]
            yo_f = jnp.dot(cg, s_f.astype(BF16), preferred_element_type=F32) * gx_f[:, gl]
            st_scr[0, :, gl] = s_f * gx_f[127:128, gl] + lax.dot_general(
                bg, xd_f[:, gl], _TN, preferred_element_type=F32)
            tb_scr[c, :, gl] = lax.dot_general(bg, xd_b[:, gl], _TN, preferred_element_type=F32)
            cdb_scr[c, 0:1, gl] = gx_b[0:1, gl]
            for pair in range(2):
                hp = 2 * g + pair
                pl_ = slice(hp * 128, (hp + 1) * 128)
                rhs = jnp.concatenate([xdt_f[:, pl_], xdt_b[:, pl_]], axis=0)
                ys = []
                for sub in range(2):
                    h = 2 * hp + sub
                    hb = SSD_HEADS + h
                    seg_f = acum[:, h:h + 1] - acum_t[h:h + 1, :]
                    m_f = jnp.exp(jnp.where(low, seg_f, NEG_BIG)) * cb
                    seg_b = acum[:, hb:hb + 1] - acum_t[hb:hb + 1, :]
                    m_b = jnp.exp(jnp.where(upp, seg_b, NEG_BIG)) * cb
                    lhs = jnp.concatenate([m_f, m_b], axis=1).astype(BF16)
                    ys.append(jnp.dot(lhs, rhs, preferred_element_type=F32))
                y_pair = jnp.where(half, ys[0], ys[1])
                y_parts.append(y_pair + yo_f[:, pair * 128:(pair + 1) * 128])
        y_ref[0, rows, :] = jnp.concatenate(y_parts, axis=1)
        return carry

    lax.fori_loop(0, nc, fwd_body, 0, unroll=2)
    if emit_state:
        for blk in range(4):
            bl = slice(blk * 128, (blk + 1) * 128)
            sf_ref[0, 0, bl, :] = st_scr[0, :, bl].T

    dexp = dexp_ref[0]
    nw = nw_ref[0]

    def bwd_body(t, carry):
        c = nc - 1 - t
        r0 = pl.multiple_of(c * SSD_CHUNK, SSD_CHUNK)
        rows = pl.ds(r0, SSD_CHUNK)
        cc = c_ref[0, rows, :]
        gx_b = gxb_scr[rows, :]
        yo = []
        for g in range(2):
            gl = slice(g * 256, (g + 1) * 256)
            cg = cc[:, g * 128:(g + 1) * 128]
            s_b = st_scr[1, :, gl]
            yo.append(jnp.dot(cg, s_b.astype(BF16), preferred_element_type=F32) * gx_b[:, gl])
            st_scr[1, :, gl] = s_b * cdb_scr[c, 0:1, gl] + tb_scr[c, :, gl]
        x = xs_ref[0, rows, :]
        y = y_ref[0, rows, :] + jnp.concatenate(yo, axis=1) + dexp * x
        y = y * _silu(z_ref[0, rows, :])
        ms = jnp.mean(y * y, axis=-1, keepdims=True)
        y_ref[0, rows, :] = y * lax.rsqrt(ms + RMS_EPS) * nw
        return carry

    lax.fori_loop(0, nc, bwd_body, 0, unroll=2)
    if emit_state:
        for blk in range(4):
            bl = slice(blk * 128, (blk + 1) * 128)
            sb_ref[0, 0, bl, :] = st_scr[1, :, bl].T


def _ssd_call(xs, bm, cm, dt, z, lw, l, init, state_bufs):
    bsz, n, _ = xs.shape
    nc = n // SSD_CHUNK
    seq = lambda w: pl.BlockSpec((1, n, w), lambda b: (b, 0, 0))
    st = pl.BlockSpec((1, 1, SSD_INNER, SSD_STATE), lambda b: (b, l, 0, 0))
    layer = lambda shp: pl.BlockSpec((1,) + shp, lambda b: (l,) + (0,) * len(shp))
    in_specs = [seq(512), seq(256), seq(256), seq(128), seq(512),
                layer((8, 128)), layer((1, 512)), layer((1, 512)),
                pl.BlockSpec((256, 3072), lambda b: (0, 0))]
    args = [xs, bm, cm, dt, z, lw["prm"], lw["dexp"], lw["ssd_nw"], lw["emat"]]
    has_init = init is not None
    emit_state = state_bufs is not None
    if has_init:
        in_specs += [st, st]
        args += list(init)
    out_shape = [jax.ShapeDtypeStruct((bsz, n, SSD_INNER), F32)]
    out_specs = [seq(512)]
    aliases = {}
    if emit_state:
        aliases = {len(args): 1, len(args) + 1: 2}
        in_specs += [pl.BlockSpec(memory_space=pl.ANY)] * 2
        args += list(state_bufs)
        out_shape += [jax.ShapeDtypeStruct(state_bufs[0].shape, F32)] * 2
        out_specs += [st, st]
    return pl.pallas_call(
        functools.partial(_ssd_kernel, nc=nc, has_init=has_init, emit_state=emit_state),
        grid=(bsz,),
        in_specs=in_specs,
        out_specs=tuple(out_specs),
        out_shape=tuple(out_shape),
        input_output_aliases=aliases,
        scratch_shapes=[pltpu.VMEM((2, SSD_STATE, SSD_INNER), F32),
                        pltpu.VMEM((n, SSD_INNER), F32),
                        pltpu.VMEM((nc, SSD_STATE, SSD_INNER), F32),
                        pltpu.VMEM((nc, 8, SSD_INNER), F32)],
        compiler_params=_params("arbitrary"),
        name="ssd_scan_init" if has_init else "ssd_scan",
    )(*args)


def _attn_kernel(*refs, lam_init, has_prefix, from_cache):
    refs = list(refs)
    q_ref, k_ref, v_ref = refs[:3]
    pos = 3
    if has_prefix:
        kp_ref, vpt_ref = refs[pos:pos + 2]
        pos += 2
    lp_ref, nw_ref, ones_ref, o_ref = refs[pos:pos + 4]

    q = q_ref[0]
    tq = q.shape[0]
    if from_cache:
        segs = [(k_ref[0, 0].astype(BF16), v_ref[0, 0].T.astype(BF16))]
    else:
        segs = [(k_ref[0], v_ref[0])]
    if has_prefix:
        segs.append((kp_ref[0, 0].astype(BF16), vpt_ref[0, 0].astype(BF16)))
    lp = lp_ref[0]
    lam = (jnp.exp(jnp.sum(lp[0:1] * lp[1:2], axis=-1, keepdims=True))
           - jnp.exp(jnp.sum(lp[2:3] * lp[3:4], axis=-1, keepdims=True)) + lam_init)
    lane = lax.broadcasted_iota(jnp.int32, (1, 128), 1)
    row = lax.broadcasted_iota(jnp.int32, (128, 1), 0)

    def scores(head):
        g, hh = divmod(head, 2)
        gs = slice(g * 128, (g + 1) * 128)
        qg = q[:, gs]
        qms = []
        for s in range(2):
            lo = (2 * hh + s) * DIFF_HD
            qms.append(jnp.where((lane >= lo) & (lane < lo + DIFF_HD), qg, jnp.zeros_like(qg)))
        qm = jnp.concatenate(qms, axis=0)
        return [lax.dot_general(kk[:, gs], qm, _NT, preferred_element_type=F32) for kk, _ in segs]

    def weighted_values(head, scs):
        g, hh = divmod(head, 2)
        gs = slice(g * 128, (g + 1) * 128)
        own = slice(DIFF_VD * hh, DIFF_VD * (hh + 1))
        other = DIFF_VD * (1 - hh)
        in_head = jnp.where((row >= DIFF_VD * hh) & (row < DIFF_VD * (hh + 1)), 1.0, 0.0).astype(BF16)
        vmod = [vt[gs, :] * in_head + (1.0 - in_head) for _, vt in segs]
        mx = jnp.max(scs[0], axis=0, keepdims=True)
        for sc in scs[1:]:
            mx = jnp.maximum(mx, jnp.max(sc, axis=0, keepdims=True))
        res = None
        for sc, vm in zip(scs, vmod):
            part = jnp.dot(vm, jnp.exp2(sc - mx).astype(BF16), preferred_element_type=F32)
            res = part if res is None else res + part
        r0, r1 = res[:, :tq], res[:, tq:]
        return r0[own] * (1.0 / r0[other:other + 1]) - r1[own] * (lam / r1[other:other + 1])

    o_heads = []
    pending = scores(0)
    for head in range(DIFF_HEADS):
        nxt = scores(head + 1) if head + 1 < DIFF_HEADS else None
        o_heads.append(weighted_values(head, pending))
        pending = nxt
    o = jnp.concatenate(o_heads, axis=0).T
    ms = jnp.dot(jnp.concatenate(_split2(o * o), axis=1), ones_ref[...],
                 preferred_element_type=F32) * (1.0 / DIFF_VD)
    o_ref[0] = o * lax.rsqrt(ms + RMS_EPS) * nw_ref[0] * (1.0 - lam_init)


def _head_ones():
    e = np.zeros((512, 256), np.float32)
    for part in range(2):
        for h in range(DIFF_HEADS):
            e[part * 256 + h * 64:part * 256 + (h + 1) * 64, h * 64:(h + 1) * 64] = 1.0
    return jnp.asarray(e, dtype=BF16)


def _attn_call(q, k, v, prefix, lw, l, lam_init, from_cache, tq=256):
    bsz, n, _ = q.shape
    layer = lambda shp: pl.BlockSpec((1,) + shp, lambda b, i: (l,) + (0,) * len(shp))
    if from_cache:
        kv_spec = pl.BlockSpec((1, 1, n, DIFF_WIDTH), lambda b, i: (b, l, 0, 0))
    else:
        kv_spec = pl.BlockSpec((1, n, DIFF_WIDTH), lambda b, i: (b, 0, 0))
    v_spec = kv_spec if from_cache else pl.BlockSpec((1, DIFF_WIDTH, n), lambda b, i: (b, 0, 0))
    in_specs = [pl.BlockSpec((1, tq, DIFF_WIDTH), lambda b, i: (b, i, 0)), kv_spec, v_spec]
    args = [q, k, v]
    has_prefix = prefix is not None
    if has_prefix:
        past = prefix[0].shape[2]
        in_specs += [pl.BlockSpec((1, 1, past, DIFF_WIDTH), lambda b, i: (b, l, 0, 0)),
                     pl.BlockSpec((1, 1, DIFF_WIDTH, past), lambda b, i: (b, l, 0, 0))]
        args += list(prefix)
    in_specs += [layer((4, DIFF_HD)), layer((1, DIFF_WIDTH)), pl.BlockSpec((512, 256), lambda b, i: (0, 0))]
    args += [lw["lp"], lw["attn_nw"], lw["ones"]]
    return pl.pallas_call(
        functools.partial(_attn_kernel, lam_init=lam_init, has_prefix=has_prefix, from_cache=from_cache),
        grid=(bsz, n // tq),
        in_specs=in_specs,
        out_specs=pl.BlockSpec((1, tq, DIFF_WIDTH), lambda b, i: (b, i, 0)),
        out_shape=jax.ShapeDtypeStruct((bsz, n, DIFF_WIDTH), F32),
        compiler_params=_params("arbitrary", "arbitrary"),
        name="diff_attn_prefix" if has_prefix else "diff_attn",
    )(*args)


def _mlp_kernel(x_ref, yssd_ref, ysc_ref, yat_ref, mod_ref, wo_ref, wu_ref, wd_ref, ln_ref, o_ref):
    x = x_ref[0]
    m = (jnp.dot(yssd_ref[0].astype(BF16), wo_ref[0, 0:512, :], preferred_element_type=F32)
         + jnp.dot(ysc_ref[0].astype(BF16), wo_ref[0, 512:768, :], preferred_element_type=F32)
         + jnp.dot(yat_ref[0].astype(BF16), wo_ref[0, 768:1024, :], preferred_element_type=F32))
    g1 = mod_ref[0, 2:3, :]
    sh2 = mod_ref[0, 3:4, :]
    sc2 = mod_ref[0, 4:5, :]
    g2 = mod_ref[0, 5:6, :]
    x1 = _layer_norm(ALPHA * x + g1 * m, ln_ref[0, 0:1, :], ln_ref[0, 1:2, :])
    h = (x1 * (1.0 + sc2) + sh2).astype(BF16)
    up = jnp.dot(h, wu_ref[0], preferred_element_type=F32)
    act = jnp.square(jnp.maximum(up, 0.0)).astype(BF16)
    f = jnp.dot(act, wd_ref[0], preferred_element_type=F32)
    o_ref[0] = _layer_norm(ALPHA * x1 + g2 * f, ln_ref[0, 2:3, :], ln_ref[0, 3:4, :])


def _mlp_call(x, yssd, ysc, yat, mod, lw, l, tm):
    bsz, n, _ = x.shape
    bm = mod.shape[0]
    mod_idx = (lambda b, i: (b, 0, 0)) if bm > 1 else (lambda b, i: (0, 0, 0))
    tok = lambda w: pl.BlockSpec((1, tm, w), lambda b, i: (b, i, 0))
    layer = lambda shp: pl.BlockSpec((1,) + shp, lambda b, i: (l,) + (0,) * len(shp),
                                     pipeline_mode=pl.Buffered(1))
    return pl.pallas_call(
        _mlp_kernel,
        grid=(bsz, n // tm),
        in_specs=[tok(D_MODEL), tok(512), tok(256), tok(256),
                  pl.BlockSpec((1, 6, D_MODEL), mod_idx),
                  layer((D_MODEL, D_MODEL)), layer((D_MODEL, D_FF)), layer((D_FF, D_MODEL)),
                  layer((4, D_MODEL))],
        out_specs=tok(D_MODEL),
        out_shape=jax.ShapeDtypeStruct((bsz, n, D_MODEL), F32),
        compiler_params=_params("arbitrary", "arbitrary"),
        name="out_mlp",
    )(x, yssd, ysc, yat, mod, lw["w_out"], lw["w_up"], lw["w_down"], lw["ln"])


def _rope_tables(n):
    rows = n // GRID_W
    row = jnp.repeat(jnp.arange(rows, dtype=F32), GRID_W)
    col = jnp.tile(jnp.arange(GRID_W, dtype=F32), rows)
    inv = ROPE_BASE ** (-jnp.arange(ROPE_F, dtype=F32) / ROPE_F)
    ang = jnp.stack([row[:, None] * inv, col[:, None] * inv], axis=1)
    ang = jnp.repeat(ang[:, :, None, :], 2, axis=2).reshape(n, DIFF_HD)
    sign = jnp.tile(jnp.concatenate([-jnp.ones(ROPE_F, F32), jnp.ones(ROPE_F, F32)]), 2)
    cos_t = jnp.tile(jnp.cos(ang), (1, 2 * DIFF_HEADS))
    sin_t = jnp.tile(jnp.sin(ang) * sign, (1, 2 * DIFF_HEADS))
    return cos_t, sin_t


def _layer_weights(w_in, ssd_conv_w, ssd_conv_b, ssd_dt_bias, ssd_a_log, ssd_d, ssd_norm_w, sc_conv_w,
                   diff_lambda, diff_norm_w, w_out, ln1_g, ln1_b, w_up, w_down, ln2_g, ln2_b):
    w_in_p = jnp.concatenate(
        [w_in[:, :, :1536], w_in[:, :, 1552:], w_in[:, :, 1536:1552],
         jnp.zeros((DEPTH, D_MODEL, IN_COLS_PAD - IN_COLS), F32)], axis=2).astype(BF16)
    prm = jnp.zeros((DEPTH, 8, 128), F32)
    prm = prm.at[:, 0, :16].set(ssd_dt_bias.reshape(DEPTH, 16)).at[:, 1, :16].set(ssd_a_log.reshape(DEPTH, 16))
    return {
        "w_in": w_in_p,
        "w_vt": jnp.swapaxes(w_in[:, :, IN_COLS - DIFF_WIDTH:], 1, 2).astype(BF16),
        "conv_w": ssd_conv_w,
        "conv_b": ssd_conv_b.reshape(DEPTH, 1, SSD_CONV_CH),
        "sc_w": sc_conv_w,
        "prm": prm,
        "dexp": jnp.repeat(ssd_d, SSD_HEAD_DIM, axis=1).reshape(DEPTH, 1, SSD_INNER),
        "ssd_nw": ssd_norm_w.reshape(DEPTH, 1, SSD_INNER),
        "emat": _expand_matrix(),
        "lp": diff_lambda,
        "attn_nw": jnp.tile(diff_norm_w, (1, DIFF_HEADS)).reshape(DEPTH, 1, DIFF_WIDTH),
        "ones": _head_ones(),
        "w_out": w_out.astype(BF16),
        "w_up": w_up.astype(BF16),
        "w_down": w_down.astype(BF16),
        "ln": jnp.stack([ln1_g, ln1_b, ln2_g, ln2_b], axis=1),
    }


def _trunk_layer(x, mod, lw, l, lam_init, rope, prefix, init, caches, state_bufs, tm):
    bsz, n, _ = x.shape
    res = _inproj_call(x, mod, lw, l, rope, caches, tm)
    z, xs, bm, cm, dt, ysc, q, k, v = res
    ssd = _ssd_call(xs, bm, cm, dt, z, lw, l, init, state_bufs)
    yat = _attn_call(q, k, v, prefix, lw, l, lam_init, from_cache=caches is not None)
    if mod.shape[0] == 1:
        flat = lambda t: t.reshape(1, bsz * n, t.shape[-1])
        y = _mlp_call(flat(x), flat(ssd[0]), flat(ysc), flat(yat), mod, lw, l, 512).reshape(bsz, n, D_MODEL)
    else:
        y = _mlp_call(x, ssd[0], ysc, yat, mod, lw, l, 512)
    new_caches = (k, v) if caches is not None else None
    new_states = (ssd[1], ssd[2]) if state_bufs is not None else None
    return y, new_caches, new_states


def kernel(x_prompt, x_sample, cache_k, cache_v, state_ssm_fwd, state_ssm_bwd, c, c_ctx, w_mod, b_mod, w_in,
           ssd_conv_w, ssd_conv_b, ssd_dt_bias, ssd_a_log, ssd_d, ssd_norm_w, sc_conv_w, diff_lambda,
           diff_norm_w, w_out, ln1_g, ln1_b, w_up, w_down, ln2_g, ln2_b):
    bp, seq, _ = x_prompt.shape
    bs, dec_seq, _ = x_sample.shape
    past = cache_k.shape[2]

    cvec = jnp.zeros((16, D_MODEL), F32).at[:bs].set(c).at[bs].set(c_ctx)
    mods = _mod_call(cvec, w_mod, b_mod)
    rope = _rope_tables(dec_seq)
    lw = _layer_weights(w_in, ssd_conv_w, ssd_conv_b, ssd_dt_bias, ssd_a_log, ssd_d, ssd_norm_w, sc_conv_w,
                        diff_lambda, diff_norm_w, w_out, ln1_g, ln1_b, w_up, w_down, ln2_g, ln2_b)
    prefix = (cache_k.reshape(bs, DEPTH, past, DIFF_WIDTH),
              jnp.swapaxes(cache_v.reshape(bs, DEPTH, past, DIFF_WIDTH), 2, 3))
    init = (state_ssm_fwd.reshape(bs, DEPTH, SSD_INNER, SSD_STATE),
            state_ssm_bwd.reshape(bs, DEPTH, SSD_INNER, SSD_STATE))
    caches = (jnp.zeros((bp, DEPTH, seq, DIFF_WIDTH), F32), jnp.zeros((bp, DEPTH, seq, DIFF_WIDTH), F32))
    states = (jnp.zeros((bp, DEPTH, SSD_INNER, SSD_STATE), F32), jnp.zeros((bp, DEPTH, SSD_INNER, SSD_STATE), F32))

    hp, hs = x_prompt, x_sample
    for l in range(DEPTH):
        lam_init = 0.8 - 0.6 * math.exp(-0.3 * l)
        mod_ctx = mods[l, bs:bs + 1].reshape(1, 6, D_MODEL)
        mod_lat = mods[l, :bs].reshape(bs, 6, D_MODEL)
        hp, caches, states = _trunk_layer(hp, mod_ctx, lw, l, lam_init, None, None, None, caches, states, seq)
        hs = _trunk_layer(hs, mod_lat, lw, l, lam_init, rope, prefix, init, None, None, 512)[0]

    return (hp, hs,
            caches[0].reshape(bp, DEPTH, seq, 2 * DIFF_HEADS, DIFF_HD),
            caches[1].reshape(bp, DEPTH, seq, DIFF_HEADS, DIFF_VD),
            states[0].reshape(bp, DEPTH, SSD_HEADS, SSD_HEAD_DIM, SSD_STATE),
            states[1].reshape(bp, DEPTH, SSD_HEADS, SSD_HEAD_DIM, SSD_STATE))
```

```python
import functools
import math

import jax
import jax.numpy as jnp
import numpy as np
from jax import lax
from jax.experimental import pallas as pl
from jax.experimental.pallas import tpu as pltpu

F32 = jnp.float32
BF16 = jnp.bfloat16

D_MODEL = 1024
DEPTH = 4
GRID_W = 64
SSD_INNER = 512
SSD_HEAD_DIM = 64
SSD_HEADS = 8
SSD_STATE = 128
SSD_CHUNK = 128
SSD_CONV = 5
SSD_CONV_CH = 1024
SC_WIDTH = 256
SC_CONV = 3
DIFF_WIDTH = 256
DIFF_HEADS = 4
DIFF_HD = 32
DIFF_VD = 64
ROPE_BASE = 10000.0
ROPE_F = 8
D_FF = 4096
ALPHA = (2 * DEPTH) ** 0.25
LN_EPS = 1e-5
RMS_EPS = 1e-6

C_Z = 0
C_XBC = 512
C_SCB = 1536
C_SCC = 1792
C_Q = 2304
C_K = 2560
C_V = 2816
C_DT = 3072
IN_COLS = 3088
IN_COLS_PAD = 3200

HALO = 8
VMEM_LIMIT = 56 * 1024 * 1024
NEG_BIG = -1e30
Q_PRESCALE = DIFF_HD ** -0.5 * math.log2(math.e)

_NT = (((1,), (1,)), ((), ()))
_TN = (((0,), (0,)), ((), ()))


def _split2(x):
    hi = x.astype(BF16)
    lo = (x - hi.astype(F32)).astype(BF16)
    return hi, lo


def _split3(x):
    hi = x.astype(BF16)
    r1 = x - hi.astype(F32)
    mid = r1.astype(BF16)
    lo = (r1 - mid.astype(F32)).astype(BF16)
    return hi, mid, lo


def _silu(x):
    return x * jax.nn.sigmoid(x)


def _layer_norm(x, g, b):
    mu = jnp.mean(x, axis=-1, keepdims=True)
    xc = x - mu
    var = jnp.mean(xc * xc, axis=-1, keepdims=True)
    return xc * lax.rsqrt(var + LN_EPS) * g + b


def _params(*sem):
    return pltpu.CompilerParams(dimension_semantics=sem, vmem_limit_bytes=VMEM_LIMIT)


def _mod_kernel(c_ref, w_ref, b_ref, o_ref):
    a = _silu(c_ref[...]).astype(BF16)
    o_ref[0] = jnp.dot(a, w_ref[0].astype(BF16), preferred_element_type=F32) + b_ref[0]


def _mod_call(cvec, w_mod, b_mod):
    tn = 1536
    return pl.pallas_call(
        _mod_kernel,
        grid=(DEPTH, 6 * D_MODEL // tn),
        in_specs=[
            pl.BlockSpec((16, D_MODEL), lambda l, j: (0, 0)),
            pl.BlockSpec((1, D_MODEL, tn), lambda l, j: (l, 0, j)),
            pl.BlockSpec((1, 1, tn), lambda l, j: (l, 0, j)),
        ],
        out_specs=pl.BlockSpec((1, 16, tn), lambda l, j: (l, 0, j)),
        out_shape=jax.ShapeDtypeStruct((DEPTH, 16, 6 * D_MODEL), F32),
        compiler_params=_params("arbitrary", "arbitrary"),
        name="mod_vectors",
    )(cvec, w_mod, b_mod.reshape(DEPTH, 1, 6 * D_MODEL))


def _inproj_kernel(*refs, tm, n_tiles, use_rope, cache_out, cache_alias):
    refs = list(refs)
    x_ref, xp_ref, xn_ref, mod_ref, w_ref, cw_ref, cb_ref, scw_ref = refs[:8]
    pos = 8
    if use_rope:
        cos_ref, sin_ref = refs[pos:pos + 2]
        pos += 2
    if cache_alias:
        pos += 2
    if not cache_out:
        wvt_ref = refs[pos]
        pos += 1
    z_ref, xs_ref, b_ref, c_ref, dt_ref, ysc_ref, q_ref, k_ref, v_ref = refs[pos:pos + 9]
    i = pl.program_id(1)
    sh1 = mod_ref[0, 0:1, :]
    one_sc1 = 1.0 + mod_ref[0, 1:2, :]
    rows_e = tm + 2 * HALO
    xe = jnp.concatenate([xp_ref[0], x_ref[0], xn_ref[0]], axis=0)
    ue = (xe * one_sc1 + sh1).astype(BF16)
    u = (x_ref[0] * one_sc1 + sh1).astype(BF16)
    rowid = lax.broadcasted_iota(jnp.int32, (rows_e, 1), 0)
    keep = jnp.where(rowid < HALO, jnp.where(i > 0, 1.0, 0.0),
                     jnp.where(rowid >= tm + HALO, jnp.where(i < n_tiles - 1, 1.0, 0.0), 1.0))

    def mm(a, lo, hi):
        return jnp.dot(a, w_ref[0, :, lo:hi], preferred_element_type=F32)

    def shifted(t, off):
        r = t if off == 0 else pltpu.roll(t, (rows_e - off) % rows_e, 0)
        return r[HALO:HALO + tm]

    z_ref[0] = mm(u, C_Z, C_XBC)
    dt_ref[0] = mm(u, C_DT, IN_COLS_PAD)

    xbc = mm(ue, C_XBC, C_SCB) * keep
    acc = cw_ref[0, 0:1, :] * shifted(xbc, -2) + cb_ref[0]
    for kk in range(1, SSD_CONV):
        acc = acc + cw_ref[0, kk:kk + 1, :] * shifted(xbc, kk - 2)
    act = _silu(acc)
    xs_ref[0] = act[:, :SSD_INNER]
    b_ref[0] = act[:, SSD_INNER:SSD_INNER + 256].astype(b_ref.dtype)
    c_ref[0] = act[:, SSD_INNER + 256:].astype(c_ref.dtype)

    scb = mm(u, C_SCB, C_SCC)
    ch = mm(ue, C_SCC, C_Q)
    prod = ch[:, :SC_WIDTH] * ch[:, SC_WIDTH:] * keep
    conv = scw_ref[0, 0:1, :] * shifted(prod, -1)
    for kk in range(1, SC_CONV):
        conv = conv + scw_ref[0, kk:kk + 1, :] * shifted(prod, kk - 1)
    ysc_ref[0] = scb * conv

    q = mm(u, C_Q, C_K)
    k = mm(u, C_K, C_V)
    if use_rope:
        cos = cos_ref[...]
        sin = sin_ref[...]
        lane = lax.broadcasted_iota(jnp.int32, (tm, DIFF_WIDTH), 1)
        first = (lane & 15) < 8

        def rope(t):
            sw = jnp.where(first, pltpu.roll(t, DIFF_WIDTH - 8, 1), pltpu.roll(t, 8, 1))
            return t * cos + sw * sin

        q = rope(q)
        k = rope(k)
    q_ref[0] = (q * Q_PRESCALE).astype(q_ref.dtype)
    if cache_out:
        k_ref[0, 0] = k
        v_ref[0, 0] = mm(u, C_V, C_DT)
    else:
        k_ref[0] = k.astype(k_ref.dtype)
        v_ref[0] = lax.dot_general(wvt_ref[0], u, _NT, preferred_element_type=F32).astype(v_ref.dtype)


def _inproj_call(x, mod, lw, l, rope, caches, tm):
    bsz, n, _ = x.shape
    n_tiles = n // tm
    tpb = tm // 8
    bm = mod.shape[0]
    mod_idx = (lambda b, i: (b, 0, 0)) if bm > 1 else (lambda b, i: (0, 0, 0))
    tok = lambda w: pl.BlockSpec((1, tm, w), lambda b, i: (b, i, 0))
    layer = lambda shp: pl.BlockSpec((1,) + shp, lambda b, i: (l,) + (0,) * len(shp))
    in_specs = [
        tok(D_MODEL),
        pl.BlockSpec((1, HALO, D_MODEL), lambda b, i: (b, jnp.maximum(i * tpb - 1, 0), 0)),
        pl.BlockSpec((1, HALO, D_MODEL), lambda b, i: (b, jnp.minimum((i + 1) * tpb, n // 8 - 1), 0)),
        pl.BlockSpec((1, 6, D_MODEL), mod_idx),
        layer((D_MODEL, IN_COLS_PAD)),
        layer((SSD_CONV, SSD_CONV_CH)),
        layer((1, SSD_CONV_CH)),
        layer((SC_CONV, SC_WIDTH)),
    ]
    args = [x, x, x, mod, lw["w_in"], lw["conv_w"], lw["conv_b"], lw["sc_w"]]
    use_rope = rope is not None
    if use_rope:
        in_specs += [pl.BlockSpec((tm, DIFF_WIDTH), lambda b, i: (i, 0))] * 2
        args += list(rope)
    sds = lambda w, dt: jax.ShapeDtypeStruct((bsz, n, w), dt)
    out_shape = [sds(512, F32), sds(512, F32), sds(256, BF16), sds(256, BF16), sds(128, F32),
                 sds(256, F32), sds(256, BF16)]
    out_specs = [tok(s.shape[-1]) for s in out_shape]
    aliases = {}
    cache_out = caches is not None
    cache_alias = cache_out and not isinstance(caches[0], jax.ShapeDtypeStruct)
    if cache_out:
        if cache_alias:
            aliases = {len(args): 7, len(args) + 1: 8}
            in_specs += [pl.BlockSpec(memory_space=pl.ANY)] * 2
            args += list(caches)
        out_shape += [jax.ShapeDtypeStruct(caches[0].shape, F32)] * 2
        out_specs += [pl.BlockSpec((1, 1, tm, DIFF_WIDTH), lambda b, i: (b, l, i, 0))] * 2
    else:
        in_specs += [layer((DIFF_WIDTH, D_MODEL))]
        args += [lw["w_vt"]]
        out_shape += [sds(256, BF16), jax.ShapeDtypeStruct((bsz, DIFF_WIDTH, n), BF16)]
        out_specs += [tok(256), pl.BlockSpec((1, DIFF_WIDTH, tm), lambda b, i: (b, 0, i))]
    return pl.pallas_call(
        functools.partial(_inproj_kernel, tm=tm, n_tiles=n_tiles, use_rope=use_rope, cache_out=cache_out,
                          cache_alias=cache_alias),
        grid=(bsz, n_tiles),
        in_specs=in_specs,
        out_specs=tuple(out_specs),
        out_shape=tuple(out_shape),
        input_output_aliases=aliases,
        compiler_params=_params("arbitrary", "arbitrary"),
        name="in_proj_rope" if use_rope else "in_proj",
    )(*args)


def _expand_matrix():
    e = np.zeros((256, 3 * 1024), np.float32)
    for part in range(2):
        for qn in range(3):
            for d in range(2):
                for h in range(SSD_HEADS):
                    r = part * 128 + qn * 16 + d * 8 + h
                    c0 = qn * 1024 + d * 512 + h * 64
                    e[r, c0:c0 + 64] = 1.0
    return jnp.asarray(e, dtype=BF16)


def _ssd_kernel(*refs, nc, has_init, emit_state, state_alias):
    refs = list(refs)
    xs_ref, b_ref, c_ref, dt_ref, z_ref, prm_ref, dexp_ref, nw_ref, emat_ref = refs[:9]
    pos = 9
    if has_init:
        s0f_ref, s0b_ref = refs[pos:pos + 2]
        pos += 2
    if state_alias:
        pos += 2
    y_ref = refs[pos]
    pos += 1
    if emit_state:
        sf_ref, sb_ref = refs[pos:pos + 2]
        pos += 2
    st_scr, gxf_scr, gxb_scr, xdtf_scr, xdtb_scr, xdf_scr, xdb_scr, acum_scr = refs[pos:pos + 8]
    grp = 4 if nc % 4 == 0 else (2 if nc % 2 == 0 else 1)

    lane = lax.broadcasted_iota(jnp.int32, (1, 128), 1)
    ii = lax.broadcasted_iota(jnp.int32, (128, 128), 0)
    jj = lax.broadcasted_iota(jnp.int32, (128, 128), 1)
    low = jj <= ii
    upp = jj >= ii
    ltri = jnp.where(low, 1.0, 0.0).astype(BF16)
    bias = prm_ref[0, 0:1, :]
    a_row = -jnp.exp(prm_ref[0, 1:2, :])
    is_head = lane < 2 * SSD_HEADS
    is_fwd = lane < SSD_HEADS
    half = lane < 64

    for blk in range(4):
        bl = slice(blk * 128, (blk + 1) * 128)
        if has_init:
            st_scr[0, :, bl] = s0f_ref[0, 0, bl, :].T
            st_scr[1, :, bl] = s0b_ref[0, 0, bl, :].T
        else:
            st_scr[0, :, bl] = jnp.zeros((SSD_STATE, 128), F32)
            st_scr[1, :, bl] = jnp.zeros((SSD_STATE, 128), F32)

    def prep_body(t, carry):
        r0 = pl.multiple_of(t * (grp * SSD_CHUNK), grp * SSD_CHUNK)
        rows = pl.ds(r0, grp * SSD_CHUNK)
        dtv = jnp.where(is_head, jax.nn.softplus(dt_ref[0, rows, :] + bias), 0.0)
        dta = dtv * a_row
        chunk = lambda v, c: v[c * SSD_CHUNK:(c + 1) * SSD_CHUNK]
        parts = []
        for c in range(grp):
            parts += list(_split3(chunk(dta, c)))
        a3 = jnp.dot(ltri, jnp.concatenate(parts, axis=1), preferred_element_type=F32)
        acums, packs = [], []
        for c in range(grp):
            o = c * 384
            cum = a3[:, o:o + 128] + a3[:, o + 128:o + 256] + a3[:, o + 256:o + 384]
            total = cum[127:128, :]
            acum = jnp.where(is_fwd, cum, total - cum + chunk(dta, c))
            gdec = jnp.where(is_head, jnp.exp(acum), 0.0)
            wdec = chunk(dtv, c) * jnp.exp(total - acum)
            acums.append(acum)
            packs.append(chunk(dtv, c) + pltpu.roll(wdec, 16, 1) + pltpu.roll(gdec, 32, 1))
        packed = jnp.concatenate(packs, axis=0)
        x6 = jnp.dot(jnp.concatenate(_split2(packed), axis=1), emat_ref[...], preferred_element_type=F32)
        x = xs_ref[0, rows, :]
        xdtf_scr[rows, :] = (x * x6[:, 0:512]).astype(BF16)
        xdtb_scr[rows, :] = (x * x6[:, 512:1024]).astype(BF16)
        xdf_scr[rows, :] = (x * x6[:, 1024:1536]).astype(BF16)
        xdb_scr[rows, :] = (x * x6[:, 1536:2048]).astype(BF16)
        gxf_scr[rows, :] = x6[:, 2048:2560]
        gxb_scr[rows, :] = x6[:, 2560:3072]
        acum_scr[rows, :] = jnp.concatenate(acums, axis=0)
        return carry

    lax.fori_loop(0, nc // grp, prep_body, 0)

    def fwd_body(c, carry):
        r0 = pl.multiple_of(c * SSD_CHUNK, SSD_CHUNK)
        rows = pl.ds(r0, SSD_CHUNK)
        acum = acum_scr[rows, :]
        acum_t = acum.T
        gx_f = gxf_scr[rows, :]
        xdt_f = xdtf_scr[rows, :]
        xdt_b = xdtb_scr[rows, :]
        xd_f = xdf_scr[rows, :]
        bc = b_ref[0, rows, :]
        cc = c_ref[0, rows, :]

        y_parts = []
        for g in range(2):
            gl = slice(g * 256, (g + 1) * 256)
            bg = bc[:, g * 128:(g + 1) * 128]
            cg = cc[:, g * 128:(g + 1) * 128]
            cb = lax.dot_general(cg, bg, _NT, preferred_element_type=F32)
            s_f = st_scr[0, :, gl]
            yo_f = jnp.dot(cg, s_f.astype(BF16), preferred_element_type=F32) * gx_f[:, gl]
            st_scr[0, :, gl] = s_f * gx_f[127:128, gl] + lax.dot_general(
                bg, xd_f[:, gl], _TN, preferred_element_type=F32)
            for pair in range(2):
                hp = 2 * g + pair
                pl_ = slice(hp * 128, (hp + 1) * 128)
                rhs = jnp.concatenate([xdt_f[:, pl_], xdt_b[:, pl_]], axis=0)
                ys = []
                for sub in range(2):
                    h = 2 * hp + sub
                    hb = SSD_HEADS + h
                    seg_f = acum[:, h:h + 1] - acum_t[h:h + 1, :]
                    m_f = jnp.exp(jnp.where(low, seg_f, NEG_BIG)) * cb
                    seg_b = acum[:, hb:hb + 1] - acum_t[hb:hb + 1, :]
                    m_b = jnp.exp(jnp.where(upp, seg_b, NEG_BIG)) * cb
                    lhs = jnp.concatenate([m_f, m_b], axis=1).astype(BF16)
                    ys.append(jnp.dot(lhs, rhs, preferred_element_type=F32))
                y_pair = jnp.where(half, ys[0], ys[1])
                y_parts.append(y_pair + yo_f[:, pair * 128:(pair + 1) * 128])
        y_ref[0, rows, :] = jnp.concatenate(y_parts, axis=1)
        return carry

    lax.fori_loop(0, nc, fwd_body, 0, unroll=2)
    if emit_state:
        for blk in range(4):
            bl = slice(blk * 128, (blk + 1) * 128)
            sf_ref[0, 0, bl, :] = st_scr[0, :, bl].T

    dexp = dexp_ref[0]
    nw = nw_ref[0]

    def bwd_body(t, carry):
        c = nc - 1 - t
        r0 = pl.multiple_of(c * SSD_CHUNK, SSD_CHUNK)
        rows = pl.ds(r0, SSD_CHUNK)
        bc = b_ref[0, rows, :]
        cc = c_ref[0, rows, :]
        gx_b = gxb_scr[rows, :]
        xd_b = xdb_scr[rows, :]
        yo = []
        for g in range(2):
            gl = slice(g * 256, (g + 1) * 256)
            bg = bc[:, g * 128:(g + 1) * 128]
            cg = cc[:, g * 128:(g + 1) * 128]
            s_b = st_scr[1, :, gl]
            yo.append(jnp.dot(cg, s_b.astype(BF16), preferred_element_type=F32) * gx_b[:, gl])
            st_scr[1, :, gl] = s_b * gx_b[0:1, gl] + lax.dot_general(
                bg, xd_b[:, gl], _TN, preferred_element_type=F32)
        x = xs_ref[0, rows, :]
        y = y_ref[0, rows, :] + jnp.concatenate(yo, axis=1) + dexp * x
        y = y * _silu(z_ref[0, rows, :])
        ms = jnp.mean(y * y, axis=-1, keepdims=True)
        y_ref[0, rows, :] = y * lax.rsqrt(ms + RMS_EPS) * nw
        return carry

    lax.fori_loop(0, nc, bwd_body, 0, unroll=2)
    if emit_state:
        for blk in range(4):
            bl = slice(blk * 128, (blk + 1) * 128)
            sb_ref[0, 0, bl, :] = st_scr[1, :, bl].T


def _ssd_call(xs, bm, cm, dt, z, lw, l, init, state_bufs):
    bsz, n, _ = xs.shape
    nc = n // SSD_CHUNK
    seq = lambda w: pl.BlockSpec((1, n, w), lambda b: (b, 0, 0))
    st = pl.BlockSpec((1, 1, SSD_INNER, SSD_STATE), lambda b: (b, l, 0, 0))
    layer = lambda shp: pl.BlockSpec((1,) + shp, lambda b: (l,) + (0,) * len(shp))
    in_specs = [seq(512), seq(256), seq(256), seq(128), seq(512),
                layer((8, 128)), layer((1, 512)), layer((1, 512)),
                pl.BlockSpec((256, 3072), lambda b: (0, 0))]
    args = [xs, bm, cm, dt, z, lw["prm"], lw["dexp"], lw["ssd_nw"], lw["emat"]]
    has_init = init is not None
    emit_state = state_bufs is not None
    if has_init:
        in_specs += [st, st]
        args += list(init)
    out_shape = [jax.ShapeDtypeStruct((bsz, n, SSD_INNER), F32)]
    out_specs = [seq(512)]
    aliases = {}
    state_alias = emit_state and not isinstance(state_bufs[0], jax.ShapeDtypeStruct)
    if emit_state:
        if state_alias:
            aliases = {len(args): 1, len(args) + 1: 2}
            in_specs += [pl.BlockSpec(memory_space=pl.ANY)] * 2
            args += list(state_bufs)
        out_shape += [jax.ShapeDtypeStruct(state_bufs[0].shape, F32)] * 2
        out_specs += [st, st]
    return pl.pallas_call(
        functools.partial(_ssd_kernel, nc=nc, has_init=has_init, emit_state=emit_state,
                          state_alias=state_alias),
        grid=(bsz,),
        in_specs=in_specs,
        out_specs=tuple(out_specs),
        out_shape=tuple(out_shape),
        input_output_aliases=aliases,
        scratch_shapes=[pltpu.VMEM((2, SSD_STATE, SSD_INNER), F32)]
                       + [pltpu.VMEM((n, SSD_INNER), F32)] * 2
                       + [pltpu.VMEM((n, SSD_INNER), BF16)] * 4
                       + [pltpu.VMEM((n, 128), F32)],
        compiler_params=_params("arbitrary"),
        name="ssd_scan_init" if has_init else "ssd_scan",
    )(*args)


def _attn_kernel(*refs, lam_init, has_prefix, from_cache):
    refs = list(refs)
    q_ref, k_ref, v_ref = refs[:3]
    pos = 3
    if has_prefix:
        kp_ref, vp_ref = refs[pos:pos + 2]
        pos += 2
    lp_ref, nw_ref, ones_ref, o_ref = refs[pos:pos + 4]

    q = q_ref[0]
    tq = q.shape[0]
    if from_cache:
        segs = [(k_ref[0, 0].astype(BF16), v_ref[0, 0].T.astype(BF16))]
    else:
        segs = [(k_ref[0], v_ref[0])]
    if has_prefix:
        segs.append((kp_ref[0, 0].astype(BF16), vp_ref[0, 0].T.astype(BF16)))
    lp = lp_ref[0]
    lam = (jnp.exp(jnp.sum(lp[0:1] * lp[1:2], axis=-1, keepdims=True))
           - jnp.exp(jnp.sum(lp[2:3] * lp[3:4], axis=-1, keepdims=True)) + lam_init)
    lane = lax.broadcasted_iota(jnp.int32, (1, 128), 1)
    row = lax.broadcasted_iota(jnp.int32, (128, 1), 0)

    def scores(head):
        g, hh = divmod(head, 2)
        gs = slice(g * 128, (g + 1) * 128)
        qg = q[:, gs]
        qms = []
        for s in range(2):
            lo = (2 * hh + s) * DIFF_HD
            qms.append(jnp.where((lane >= lo) & (lane < lo + DIFF_HD), qg, jnp.zeros_like(qg)))
        qm = jnp.concatenate(qms, axis=0)
        return [lax.dot_general(kk[:, gs], qm, _NT, preferred_element_type=F32) for kk, _ in segs]

    def weighted_values(head, scs):
        g, hh = divmod(head, 2)
        gs = slice(g * 128, (g + 1) * 128)
        own = slice(DIFF_VD * hh, DIFF_VD * (hh + 1))
        other = DIFF_VD * (1 - hh)
        in_head = jnp.where((row >= DIFF_VD * hh) & (row < DIFF_VD * (hh + 1)), 1.0, 0.0).astype(BF16)
        vmod = [vt[gs, :] * in_head + (1.0 - in_head) for _, vt in segs]
        mx = jnp.max(scs[0], axis=0, keepdims=True)
        for sc in scs[1:]:
            mx = jnp.maximum(mx, jnp.max(sc, axis=0, keepdims=True))
        res = None
        for sc, vm in zip(scs, vmod):
            part = jnp.dot(vm, jnp.exp2(sc - mx).astype(BF16), preferred_element_type=F32)
            res = part if res is None else res + part
        r0, r1 = res[:, :tq], res[:, tq:]
        return r0[own] * (1.0 / r0[other:other + 1]) - r1[own] * (lam / r1[other:other + 1])

    o_heads = []
    pending = scores(0)
    for head in range(DIFF_HEADS):
        nxt = scores(head + 1) if head + 1 < DIFF_HEADS else None
        o_heads.append(weighted_values(head, pending))
        pending = nxt
    o = jnp.concatenate(o_heads, axis=0).T
    ms = jnp.dot(jnp.concatenate(_split2(o * o), axis=1), ones_ref[...],
                 preferred_element_type=F32) * (1.0 / DIFF_VD)
    o_ref[0] = o * lax.rsqrt(ms + RMS_EPS) * nw_ref[0] * (1.0 - lam_init)


def _head_ones():
    e = np.zeros((512, 256), np.float32)
    for part in range(2):
        for h in range(DIFF_HEADS):
            e[part * 256 + h * 64:part * 256 + (h + 1) * 64, h * 64:(h + 1) * 64] = 1.0
    return jnp.asarray(e, dtype=BF16)


def _attn_call(q, k, v, prefix, lw, l, lam_init, from_cache, tq=256):
    bsz, n, _ = q.shape
    layer = lambda shp: pl.BlockSpec((1,) + shp, lambda b, i: (l,) + (0,) * len(shp))
    if from_cache:
        kv_spec = pl.BlockSpec((1, 1, n, DIFF_WIDTH), lambda b, i: (b, l, 0, 0))
    else:
        kv_spec = pl.BlockSpec((1, n, DIFF_WIDTH), lambda b, i: (b, 0, 0))
    v_spec = kv_spec if from_cache else pl.BlockSpec((1, DIFF_WIDTH, n), lambda b, i: (b, 0, 0))
    in_specs = [pl.BlockSpec((1, tq, DIFF_WIDTH), lambda b, i: (b, i, 0)), kv_spec, v_spec]
    args = [q, k, v]
    has_prefix = prefix is not None
    if has_prefix:
        past = prefix[0].shape[2]
        in_specs += [pl.BlockSpec((1, 1, past, DIFF_WIDTH), lambda b, i: (b, l, 0, 0))] * 2
        args += list(prefix)
    in_specs += [layer((4, DIFF_HD)), layer((1, DIFF_WIDTH)), pl.BlockSpec((512, 256), lambda b, i: (0, 0))]
    args += [lw["lp"], lw["attn_nw"], lw["ones"]]
    return pl.pallas_call(
        functools.partial(_attn_kernel, lam_init=lam_init, has_prefix=has_prefix, from_cache=from_cache),
        grid=(bsz, n // tq),
        in_specs=in_specs,
        out_specs=pl.BlockSpec((1, tq, DIFF_WIDTH), lambda b, i: (b, i, 0)),
        out_shape=jax.ShapeDtypeStruct((bsz, n, DIFF_WIDTH), F32),
        compiler_params=_params("arbitrary", "arbitrary"),
        name="diff_attn_prefix" if has_prefix else "diff_attn",
    )(*args)


def _mlp_kernel(x_ref, yssd_ref, ysc_ref, yat_ref, mod_ref, wo_ref, wu_ref, wd_ref, ln_ref, o_ref, *, parts):
    g1 = mod_ref[0, 2:3, :]
    sh2 = mod_ref[0, 3:4, :]
    sc2 = mod_ref[0, 4:5, :]
    g2 = mod_ref[0, 5:6, :]
    rows_per = x_ref.shape[1] // parts

    def mix_norm(r):
        rows = slice(r * rows_per, (r + 1) * rows_per)
        m = (jnp.dot(yssd_ref[0, rows, :].astype(BF16), wo_ref[0, 0:512, :], preferred_element_type=F32)
             + jnp.dot(ysc_ref[0, rows, :].astype(BF16), wo_ref[0, 512:768, :], preferred_element_type=F32)
             + jnp.dot(yat_ref[0, rows, :].astype(BF16), wo_ref[0, 768:1024, :], preferred_element_type=F32))
        x1 = _layer_norm(ALPHA * x_ref[0, rows, :] + g1 * m, ln_ref[0, 0:1, :], ln_ref[0, 1:2, :])
        return x1, (x1 * (1.0 + sc2) + sh2).astype(BF16)

    def mlp_norm(r, x1, h):
        rows = slice(r * rows_per, (r + 1) * rows_per)
        up = jnp.dot(h, wu_ref[0], preferred_element_type=F32)
        act = jnp.square(jnp.maximum(up, 0.0)).astype(BF16)
        f = jnp.dot(act, wd_ref[0], preferred_element_type=F32)
        o_ref[0, rows, :] = _layer_norm(ALPHA * x1 + g2 * f, ln_ref[0, 2:3, :], ln_ref[0, 3:4, :])

    staged = mix_norm(0)
    for r in range(parts):
        nxt = mix_norm(r + 1) if r + 1 < parts else None
        mlp_norm(r, *staged)
        staged = nxt


def _mlp_call(x, yssd, ysc, yat, mod, lw, l, tm):
    bsz, n, _ = x.shape
    bm = mod.shape[0]
    mod_idx = (lambda b, i: (b, 0, 0)) if bm > 1 else (lambda b, i: (0, 0, 0))
    tok = lambda w: pl.BlockSpec((1, tm, w), lambda b, i: (b, i, 0))
    layer = lambda shp: pl.BlockSpec((1,) + shp, lambda b, i: (l,) + (0,) * len(shp),
                                     pipeline_mode=pl.Buffered(1))
    return pl.pallas_call(
        functools.partial(_mlp_kernel, parts=2),
        grid=(bsz, n // tm),
        in_specs=[tok(D_MODEL), tok(512), tok(256), tok(256),
                  pl.BlockSpec((1, 6, D_MODEL), mod_idx),
                  layer((D_MODEL, D_MODEL)), layer((D_MODEL, D_FF)), layer((D_FF, D_MODEL)),
                  layer((4, D_MODEL))],
        out_specs=tok(D_MODEL),
        out_shape=jax.ShapeDtypeStruct((bsz, n, D_MODEL), F32),
        compiler_params=_params("arbitrary", "arbitrary"),
        name="out_mlp",
    )(x, yssd, ysc, yat, mod, lw["w_out"], lw["w_up"], lw["w_down"], lw["ln"])


def _rope_tables(n):
    rows = n // GRID_W
    row = jnp.repeat(jnp.arange(rows, dtype=F32), GRID_W)
    col = jnp.tile(jnp.arange(GRID_W, dtype=F32), rows)
    inv = ROPE_BASE ** (-jnp.arange(ROPE_F, dtype=F32) / ROPE_F)
    ang = jnp.stack([row[:, None] * inv, col[:, None] * inv], axis=1)
    ang = jnp.repeat(ang[:, :, None, :], 2, axis=2).reshape(n, DIFF_HD)
    sign = jnp.tile(jnp.concatenate([-jnp.ones(ROPE_F, F32), jnp.ones(ROPE_F, F32)]), 2)
    cos_t = jnp.tile(jnp.cos(ang), (1, 2 * DIFF_HEADS))
    sin_t = jnp.tile(jnp.sin(ang) * sign, (1, 2 * DIFF_HEADS))
    return cos_t, sin_t


def _layer_weights(w_in, ssd_conv_w, ssd_conv_b, ssd_dt_bias, ssd_a_log, ssd_d, ssd_norm_w, sc_conv_w,
                   diff_lambda, diff_norm_w, w_out, ln1_g, ln1_b, w_up, w_down, ln2_g, ln2_b):
    w_in_p = jnp.concatenate(
        [w_in[:, :, :1536], w_in[:, :, 1552:], w_in[:, :, 1536:1552],
         jnp.zeros((DEPTH, D_MODEL, IN_COLS_PAD - IN_COLS), F32)], axis=2).astype(BF16)
    prm = jnp.zeros((DEPTH, 8, 128), F32)
    prm = prm.at[:, 0, :16].set(ssd_dt_bias.reshape(DEPTH, 16)).at[:, 1, :16].set(ssd_a_log.reshape(DEPTH, 16))
    return {
        "w_in": w_in_p,
        "w_vt": jnp.swapaxes(w_in[:, :, IN_COLS - DIFF_WIDTH:], 1, 2).astype(BF16),
        "conv_w": ssd_conv_w,
        "conv_b": ssd_conv_b.reshape(DEPTH, 1, SSD_CONV_CH),
        "sc_w": sc_conv_w,
        "prm": prm,
        "dexp": jnp.repeat(ssd_d, SSD_HEAD_DIM, axis=1).reshape(DEPTH, 1, SSD_INNER),
        "ssd_nw": ssd_norm_w.reshape(DEPTH, 1, SSD_INNER),
        "emat": _expand_matrix(),
        "lp": diff_lambda,
        "attn_nw": jnp.tile(diff_norm_w, (1, DIFF_HEADS)).reshape(DEPTH, 1, DIFF_WIDTH),
        "ones": _head_ones(),
        "w_out": w_out.astype(BF16),
        "w_up": w_up.astype(BF16),
        "w_down": w_down.astype(BF16),
        "ln": jnp.stack([ln1_g, ln1_b, ln2_g, ln2_b], axis=1),
    }


def _trunk_layer(x, mod, lw, l, lam_init, rope, prefix, init, caches, state_bufs, tm):
    bsz, n, _ = x.shape
    res = _inproj_call(x, mod, lw, l, rope, caches, tm)
    z, xs, bm, cm, dt, ysc, q, k, v = res
    ssd = _ssd_call(xs, bm, cm, dt, z, lw, l, init, state_bufs)
    yat = _attn_call(q, k, v, prefix, lw, l, lam_init, from_cache=caches is not None)
    if mod.shape[0] == 1:
        flat = lambda t: t.reshape(1, bsz * n, t.shape[-1])
        y = _mlp_call(flat(x), flat(ssd[0]), flat(ysc), flat(yat), mod, lw, l, 512).reshape(bsz, n, D_MODEL)
    else:
        y = _mlp_call(x, ssd[0], ysc, yat, mod, lw, l, 512)
    new_caches = (k, v) if caches is not None else None
    new_states = (ssd[1], ssd[2]) if state_bufs is not None else None
    return y, new_caches, new_states


def kernel(x_prompt, x_sample, cache_k, cache_v, state_ssm_fwd, state_ssm_bwd, c, c_ctx, w_mod, b_mod, w_in,
           ssd_conv_w, ssd_conv_b, ssd_dt_bias, ssd_a_log, ssd_d, ssd_norm_w, sc_conv_w, diff_lambda,
           diff_norm_w, w_out, ln1_g, ln1_b, w_up, w_down, ln2_g, ln2_b):
    bp, seq, _ = x_prompt.shape
    bs, dec_seq, _ = x_sample.shape
    past = cache_k.shape[2]

    cvec = jnp.zeros((16, D_MODEL), F32).at[:bs].set(c).at[bs].set(c_ctx)
    mods = _mod_call(cvec, w_mod, b_mod)
    rope = _rope_tables(dec_seq)
    lw = _layer_weights(w_in, ssd_conv_w, ssd_conv_b, ssd_dt_bias, ssd_a_log, ssd_d, ssd_norm_w, sc_conv_w,
                        diff_lambda, diff_norm_w, w_out, ln1_g, ln1_b, w_up, w_down, ln2_g, ln2_b)
    prefix = (cache_k.reshape(bs, DEPTH, past, DIFF_WIDTH), cache_v.reshape(bs, DEPTH, past, DIFF_WIDTH))
    init = (state_ssm_fwd.reshape(bs, DEPTH, SSD_INNER, SSD_STATE),
            state_ssm_bwd.reshape(bs, DEPTH, SSD_INNER, SSD_STATE))
    caches = (jax.ShapeDtypeStruct((bp, DEPTH, seq, DIFF_WIDTH), F32),) * 2
    states = (jax.ShapeDtypeStruct((bp, DEPTH, SSD_INNER, SSD_STATE), F32),) * 2

    hp, hs = x_prompt, x_sample
    for l in range(DEPTH):
        lam_init = 0.8 - 0.6 * math.exp(-0.3 * l)
        mod_ctx = mods[l, bs:bs + 1].reshape(1, 6, D_MODEL)
        mod_lat = mods[l, :bs].reshape(bs, 6, D_MODEL)
        hp, caches, states = _trunk_layer(hp, mod_ctx, lw, l, lam_init, None, None, None, caches, states, seq)
        hs = _trunk_layer(hs, mod_lat, lw, l, lam_init, rope, prefix, init, None, None, 512)[0]

    return (hp, hs,
            caches[0].reshape(bp, DEPTH, seq, 2 * DIFF_HEADS, DIFF_HD),
            caches[1].reshape(bp, DEPTH, seq, DIFF_HEADS, DIFF_VD),
            states[0].reshape(bp, DEPTH, SSD_HEADS, SSD_HEAD_DIM, SSD_STATE),
            states[1].reshape(bp, DEPTH, SSD_HEADS, SSD_HEAD_DIM, SSD_STATE))
```

```python
import functools
import math

import jax
import jax.numpy as jnp
import numpy as np
from jax import lax
from jax.experimental import pallas as pl
from jax.experimental.pallas import tpu as pltpu

F32 = jnp.float32
BF16 = jnp.bfloat16

D_MODEL = 1024
DEPTH = 4
GRID_W = 64
SSD_INNER = 512
SSD_HEAD_DIM = 64
SSD_HEADS = 8
SSD_STATE = 128
SSD_CHUNK = 128
SSD_CONV = 5
SSD_CONV_CH = 1024
SC_WIDTH = 256
SC_CONV = 3
DIFF_WIDTH = 256
DIFF_HEADS = 4
DIFF_HD = 32
DIFF_VD = 64
ROPE_BASE = 10000.0
ROPE_F = 8
D_FF = 4096
ALPHA = (2 * DEPTH) ** 0.25
LN_EPS = 1e-5
RMS_EPS = 1e-6

C_Z = 0
C_XBC = 512
C_SCB = 1536
C_SCC = 1792
C_Q = 2304
C_K = 2560
C_V = 2816
C_DT = 3072
IN_COLS = 3088
IN_COLS_PAD = 3200

HALO = 8
VMEM_LIMIT = 56 * 1024 * 1024
NEG_BIG = -1e30
Q_PRESCALE = DIFF_HD ** -0.5 * math.log2(math.e)

_NT = (((1,), (1,)), ((), ()))
_TN = (((0,), (0,)), ((), ()))


def _split2(x):
    hi = x.astype(BF16)
    lo = (x - hi.astype(F32)).astype(BF16)
    return hi, lo


def _split3(x):
    hi = x.astype(BF16)
    r1 = x - hi.astype(F32)
    mid = r1.astype(BF16)
    lo = (r1 - mid.astype(F32)).astype(BF16)
    return hi, mid, lo


def _silu(x):
    return x * jax.nn.sigmoid(x)


def _layer_norm(x, g, b):
    mu = jnp.mean(x, axis=-1, keepdims=True)
    xc = x - mu
    var = jnp.mean(xc * xc, axis=-1, keepdims=True)
    return xc * lax.rsqrt(var + LN_EPS) * g + b


def _params(*sem):
    return pltpu.CompilerParams(dimension_semantics=sem, vmem_limit_bytes=VMEM_LIMIT)


def _mod_kernel(c_ref, w_ref, b_ref, o_ref):
    a = _silu(c_ref[...]).astype(BF16)
    o_ref[0] = jnp.dot(a, w_ref[0].astype(BF16), preferred_element_type=F32) + b_ref[0]


def _mod_call(cvec, w_mod, b_mod):
    tn = 1536
    return pl.pallas_call(
        _mod_kernel,
        grid=(DEPTH, 6 * D_MODEL // tn),
        in_specs=[
            pl.BlockSpec((16, D_MODEL), lambda l, j: (0, 0)),
            pl.BlockSpec((1, D_MODEL, tn), lambda l, j: (l, 0, j)),
            pl.BlockSpec((1, 1, tn), lambda l, j: (l, 0, j)),
        ],
        out_specs=pl.BlockSpec((1, 16, tn), lambda l, j: (l, 0, j)),
        out_shape=jax.ShapeDtypeStruct((DEPTH, 16, 6 * D_MODEL), F32),
        compiler_params=_params("arbitrary", "arbitrary"),
        name="mod_vectors",
    )(cvec, w_mod, b_mod.reshape(DEPTH, 1, 6 * D_MODEL))


def _permute_kernel(w_ref, o_ref):
    w = w_ref[0]
    rows = w.shape[0]
    o_ref[0, :, 0:C_SCB] = w[:, 0:C_SCB].astype(BF16)
    o_ref[0, :, C_SCB:C_DT] = w[:, C_SCB + 16:IN_COLS].astype(BF16)
    o_ref[0, :, C_DT:IN_COLS_PAD] = jnp.concatenate(
        [w[:, C_SCB:C_SCB + 16], jnp.zeros((rows, IN_COLS_PAD - IN_COLS), F32)], axis=1).astype(BF16)


def _permute_w_in(w_in):
    tr = 256
    return pl.pallas_call(
        _permute_kernel,
        grid=(DEPTH, D_MODEL // tr),
        in_specs=[pl.BlockSpec((1, tr, IN_COLS), lambda l, i: (l, i, 0))],
        out_specs=pl.BlockSpec((1, tr, IN_COLS_PAD), lambda l, i: (l, i, 0)),
        out_shape=jax.ShapeDtypeStruct((DEPTH, D_MODEL, IN_COLS_PAD), BF16),
        compiler_params=_params("arbitrary", "arbitrary"),
        name="permute_w_in",
    )(w_in)


def _inproj_kernel(*refs, tm, n_tiles, use_rope, cache_out, cache_alias):
    refs = list(refs)
    x_ref, xp_ref, xn_ref, mod_ref, w_ref, cw_ref, cb_ref, scw_ref = refs[:8]
    pos = 8
    if use_rope:
        cos_ref, sin_ref = refs[pos:pos + 2]
        pos += 2
    if cache_alias:
        pos += 2
    if not cache_out:
        wvt_ref = refs[pos]
        pos += 1
    z_ref, xs_ref, b_ref, c_ref, dt_ref, ysc_ref, q_ref, k_ref, v_ref = refs[pos:pos + 9]
    i = pl.program_id(1)
    sh1 = mod_ref[0, 0:1, :]
    one_sc1 = 1.0 + mod_ref[0, 1:2, :]
    rows_e = tm + 2 * HALO
    xe = jnp.concatenate([xp_ref[0], x_ref[0], xn_ref[0]], axis=0)
    ue = (xe * one_sc1 + sh1).astype(BF16)
    u = (x_ref[0] * one_sc1 + sh1).astype(BF16)
    rowid = lax.broadcasted_iota(jnp.int32, (rows_e, 1), 0)
    keep = jnp.where(rowid < HALO, jnp.where(i > 0, 1.0, 0.0),
                     jnp.where(rowid >= tm + HALO, jnp.where(i < n_tiles - 1, 1.0, 0.0), 1.0))

    def mm(a, lo, hi):
        return jnp.dot(a, w_ref[0, :, lo:hi], preferred_element_type=F32)

    def shifted(t, off):
        r = t if off == 0 else pltpu.roll(t, (rows_e - off) % rows_e, 0)
        return r[HALO:HALO + tm]

    z_ref[0] = mm(u, C_Z, C_XBC)
    dt_ref[0] = mm(u, C_DT, IN_COLS_PAD)

    xbc = mm(ue, C_XBC, C_SCB) * keep
    acc = cw_ref[0, 0:1, :] * shifted(xbc, -2) + cb_ref[0]
    for kk in range(1, SSD_CONV):
        acc = acc + cw_ref[0, kk:kk + 1, :] * shifted(xbc, kk - 2)
    act = _silu(acc)
    xs_ref[0] = act[:, :SSD_INNER]
    b_ref[0] = act[:, SSD_INNER:SSD_INNER + 256].astype(b_ref.dtype)
    c_ref[0] = act[:, SSD_INNER + 256:].astype(c_ref.dtype)

    scb = mm(u, C_SCB, C_SCC)
    ch = mm(ue, C_SCC, C_Q)
    prod = ch[:, :SC_WIDTH] * ch[:, SC_WIDTH:] * keep
    conv = scw_ref[0, 0:1, :] * shifted(prod, -1)
    for kk in range(1, SC_CONV):
        conv = conv + scw_ref[0, kk:kk + 1, :] * shifted(prod, kk - 1)
    ysc_ref[0] = scb * conv

    q = mm(u, C_Q, C_K)
    k = mm(u, C_K, C_V)
    if use_rope:
        cos = cos_ref[...]
        sin = sin_ref[...]
        lane = lax.broadcasted_iota(jnp.int32, (tm, DIFF_WIDTH), 1)
        first = (lane & 15) < 8

        def rope(t):
            sw = jnp.where(first, pltpu.roll(t, DIFF_WIDTH - 8, 1), pltpu.roll(t, 8, 1))
            return t * cos + sw * sin

        q = rope(q)
        k = rope(k)
    q_ref[0] = (q * Q_PRESCALE).astype(q_ref.dtype)
    if cache_out:
        k_ref[0, 0] = k
        v_ref[0, 0] = mm(u, C_V, C_DT)
    else:
        k_ref[0] = k.astype(k_ref.dtype)
        v_ref[0] = lax.dot_general(wvt_ref[0], u, _NT, preferred_element_type=F32).astype(v_ref.dtype)


def _inproj_call(x, mod, lw, l, rope, caches, tm):
    bsz, n, _ = x.shape
    n_tiles = n // tm
    tpb = tm // 8
    bm = mod.shape[0]
    mod_idx = (lambda b, i: (b, 0, 0)) if bm > 1 else (lambda b, i: (0, 0, 0))
    tok = lambda w: pl.BlockSpec((1, tm, w), lambda b, i: (b, i, 0))
    layer = lambda shp: pl.BlockSpec((1,) + shp, lambda b, i: (l,) + (0,) * len(shp))
    in_specs = [
        tok(D_MODEL),
        pl.BlockSpec((1, HALO, D_MODEL), lambda b, i: (b, jnp.maximum(i * tpb - 1, 0), 0)),
        pl.BlockSpec((1, HALO, D_MODEL), lambda b, i: (b, jnp.minimum((i + 1) * tpb, n // 8 - 1), 0)),
        pl.BlockSpec((1, 6, D_MODEL), mod_idx),
        layer((D_MODEL, IN_COLS_PAD)),
        layer((SSD_CONV, SSD_CONV_CH)),
        layer((1, SSD_CONV_CH)),
        layer((SC_CONV, SC_WIDTH)),
    ]
    args = [x, x, x, mod, lw["w_in"], lw["conv_w"], lw["conv_b"], lw["sc_w"]]
    use_rope = rope is not None
    if use_rope:
        in_specs += [pl.BlockSpec((tm, DIFF_WIDTH), lambda b, i: (i, 0))] * 2
        args += list(rope)
    sds = lambda w, dt: jax.ShapeDtypeStruct((bsz, n, w), dt)
    out_shape = [sds(512, F32), sds(512, F32), sds(256, BF16), sds(256, BF16), sds(128, F32),
                 sds(256, F32), sds(256, BF16)]
    out_specs = [tok(s.shape[-1]) for s in out_shape]
    aliases = {}
    cache_out = caches is not None
    cache_alias = cache_out and not isinstance(caches[0], jax.ShapeDtypeStruct)
    if cache_out:
        if cache_alias:
            aliases = {len(args): 7, len(args) + 1: 8}
            in_specs += [pl.BlockSpec(memory_space=pl.ANY)] * 2
            args += list(caches)
        out_shape += [jax.ShapeDtypeStruct(caches[0].shape, F32)] * 2
        out_specs += [pl.BlockSpec((1, 1, tm, DIFF_WIDTH), lambda b, i: (b, l, i, 0))] * 2
    else:
        in_specs += [layer((DIFF_WIDTH, D_MODEL))]
        args += [lw["w_vt"]]
        out_shape += [sds(256, BF16), jax.ShapeDtypeStruct((bsz, DIFF_WIDTH, n), BF16)]
        out_specs += [tok(256), pl.BlockSpec((1, DIFF_WIDTH, tm), lambda b, i: (b, 0, i))]
    return pl.pallas_call(
        functools.partial(_inproj_kernel, tm=tm, n_tiles=n_tiles, use_rope=use_rope, cache_out=cache_out,
                          cache_alias=cache_alias),
        grid=(bsz, n_tiles),
        in_specs=in_specs,
        out_specs=tuple(out_specs),
        out_shape=tuple(out_shape),
        input_output_aliases=aliases,
        compiler_params=_params("arbitrary", "arbitrary"),
        name="in_proj_rope" if use_rope else "in_proj",
    )(*args)


def _expand_matrix():
    e = np.zeros((256, 3 * 1024), np.float32)
    for part in range(2):
        for qn in range(3):
            for d in range(2):
                for h in range(SSD_HEADS):
                    r = part * 128 + qn * 16 + d * 8 + h
                    c0 = qn * 1024 + d * 512 + h * 64
                    e[r, c0:c0 + 64] = 1.0
    return jnp.asarray(e, dtype=BF16)


def _ssd_kernel(*refs, nc, has_init, emit_state, state_alias):
    refs = list(refs)
    xs_ref, b_ref, c_ref, dt_ref, z_ref, prm_ref, dexp_ref, nw_ref, emat_ref = refs[:9]
    pos = 9
    if has_init:
        s0f_ref, s0b_ref = refs[pos:pos + 2]
        pos += 2
    if state_alias:
        pos += 2
    y_ref = refs[pos]
    pos += 1
    if emit_state:
        sf_ref, sb_ref = refs[pos:pos + 2]
        pos += 2
    st_scr, gxf_scr, gxb_scr, xdtf_scr, xdtb_scr, xdf_scr, xdb_scr, acum_scr = refs[pos:pos + 8]
    grp = 4 if nc % 4 == 0 else (2 if nc % 2 == 0 else 1)

    lane = lax.broadcasted_iota(jnp.int32, (1, 128), 1)
    ii = lax.broadcasted_iota(jnp.int32, (128, 128), 0)
    jj = lax.broadcasted_iota(jnp.int32, (128, 128), 1)
    low = jj <= ii
    upp = jj >= ii
    ltri = jnp.where(low, 1.0, 0.0).astype(BF16)
    bias = prm_ref[0, 0:1, :]
    a_row = -jnp.exp(prm_ref[0, 1:2, :])
    is_head = lane < 2 * SSD_HEADS
    is_fwd = lane < SSD_HEADS
    half = lane < 64

    for blk in range(4):
        bl = slice(blk * 128, (blk + 1) * 128)
        if has_init:
            st_scr[0, :, bl] = s0f_ref[0, 0, bl, :].T
            st_scr[1, :, bl] = s0b_ref[0, 0, bl, :].T
        else:
            st_scr[0, :, bl] = jnp.zeros((SSD_STATE, 128), F32)
            st_scr[1, :, bl] = jnp.zeros((SSD_STATE, 128), F32)

    def prep_body(t, carry):
        r0 = pl.multiple_of(t * (grp * SSD_CHUNK), grp * SSD_CHUNK)
        rows = pl.ds(r0, grp * SSD_CHUNK)
        dtv = jnp.where(is_head, jax.nn.softplus(dt_ref[0, rows, :] + bias), 0.0)
        dta = dtv * a_row
        chunk = lambda v, c: v[c * SSD_CHUNK:(c + 1) * SSD_CHUNK]
        parts = []
        for c in range(grp):
            parts += list(_split3(chunk(dta, c)))
        a3 = jnp.dot(ltri, jnp.concatenate(parts, axis=1), preferred_element_type=F32)
        acums, packs = [], []
        for c in range(grp):
            o = c * 384
            cum = a3[:, o:o + 128] + a3[:, o + 128:o + 256] + a3[:, o + 256:o + 384]
            total = cum[127:128, :]
            acum = jnp.where(is_fwd, cum, total - cum + chunk(dta, c))
            gdec = jnp.where(is_head, jnp.exp(acum), 0.0)
            wdec = chunk(dtv, c) * jnp.exp(total - acum)
            acums.append(acum)
            packs.append(chunk(dtv, c) + pltpu.roll(wdec, 16, 1) + pltpu.roll(gdec, 32, 1))
        packed = jnp.concatenate(packs, axis=0)
        x6 = jnp.dot(jnp.concatenate(_split2(packed), axis=1), emat_ref[...], preferred_element_type=F32)
        x = xs_ref[0, rows, :]
        xdtf_scr[rows, :] = (x * x6[:, 0:512]).astype(BF16)
        xdtb_scr[rows, :] = (x * x6[:, 512:1024]).astype(BF16)
        xdf_scr[rows, :] = (x * x6[:, 1024:1536]).astype(BF16)
        xdb_scr[rows, :] = (x * x6[:, 1536:2048]).astype(BF16)
        gxf_scr[rows, :] = x6[:, 2048:2560]
        gxb_scr[rows, :] = x6[:, 2560:3072]
        acum_scr[rows, :] = jnp.concatenate(acums, axis=0)
        return carry

    lax.fori_loop(0, nc // grp, prep_body, 0)

    def fwd_body(c, carry):
        r0 = pl.multiple_of(c * SSD_CHUNK, SSD_CHUNK)
        rows = pl.ds(r0, SSD_CHUNK)
        acum = acum_scr[rows, :]
        acum_t = acum.T
        gx_f = gxf_scr[rows, :]
        xdt_f = xdtf_scr[rows, :]
        xdt_b = xdtb_scr[rows, :]
        xd_f = xdf_scr[rows, :]
        bc = b_ref[0, rows, :]
        cc = c_ref[0, rows, :]

        y_parts = []
        for g in range(2):
            gl = slice(g * 256, (g + 1) * 256)
            bg = bc[:, g * 128:(g + 1) * 128]
            cg = cc[:, g * 128:(g + 1) * 128]
            cb = lax.dot_general(cg, bg, _NT, preferred_element_type=F32)
            s_f = st_scr[0, :, gl]
            yo_f = jnp.dot(cg, s_f.astype(BF16), preferred_element_type=F32) * gx_f[:, gl]
            st_scr[0, :, gl] = s_f * gx_f[127:128, gl] + lax.dot_general(
                bg, xd_f[:, gl], _TN, preferred_element_type=F32)
            for pair in range(2):
                hp = 2 * g + pair
                pl_ = slice(hp * 128, (hp + 1) * 128)
                rhs = jnp.concatenate([xdt_f[:, pl_], xdt_b[:, pl_]], axis=0)
                ys = []
                for sub in range(2):
                    h = 2 * hp + sub
                    hb = SSD_HEADS + h
                    seg_f = acum[:, h:h + 1] - acum_t[h:h + 1, :]
                    m_f = jnp.exp(jnp.where(low, seg_f, NEG_BIG)) * cb
                    seg_b = acum[:, hb:hb + 1] - acum_t[hb:hb + 1, :]
                    m_b = jnp.exp(jnp.where(upp, seg_b, NEG_BIG)) * cb
                    lhs = jnp.concatenate([m_f, m_b], axis=1).astype(BF16)
                    ys.append(jnp.dot(lhs, rhs, preferred_element_type=F32))
                y_pair = jnp.where(half, ys[0], ys[1])
                y_parts.append(y_pair + yo_f[:, pair * 128:(pair + 1) * 128])
        y_ref[0, rows, :] = jnp.concatenate(y_parts, axis=1)
        return carry

    lax.fori_loop(0, nc, fwd_body, 0, unroll=2)
    if emit_state:
        for blk in range(4):
            bl = slice(blk * 128, (blk + 1) * 128)
            sf_ref[0, 0, bl, :] = st_scr[0, :, bl].T

    dexp = dexp_ref[0]
    nw = nw_ref[0]

    def bwd_body(t, carry):
        c = nc - 1 - t
        r0 = pl.multiple_of(c * SSD_CHUNK, SSD_CHUNK)
        rows = pl.ds(r0, SSD_CHUNK)
        bc = b_ref[0, rows, :]
        cc = c_ref[0, rows, :]
        gx_b = gxb_scr[rows, :]
        xd_b = xdb_scr[rows, :]
        yo = []
        for g in range(2):
            gl = slice(g * 256, (g + 1) * 256)
            bg = bc[:, g * 128:(g + 1) * 128]
            cg = cc[:, g * 128:(g + 1) * 128]
            s_b = st_scr[1, :, gl]
            yo.append(jnp.dot(cg, s_b.astype(BF16), preferred_element_type=F32) * gx_b[:, gl])
            st_scr[1, :, gl] = s_b * gx_b[0:1, gl] + lax.dot_general(
                bg, xd_b[:, gl], _TN, preferred_element_type=F32)
        x = xs_ref[0, rows, :]
        y = y_ref[0, rows, :] + jnp.concatenate(yo, axis=1) + dexp * x
        y = y * _silu(z_ref[0, rows, :])
        ms = jnp.mean(y * y, axis=-1, keepdims=True)
        y_ref[0, rows, :] = y * lax.rsqrt(ms + RMS_EPS) * nw
        return carry

    lax.fori_loop(0, nc, bwd_body, 0, unroll=2)
    if emit_state:
        for blk in range(4):
            bl = slice(blk * 128, (blk + 1) * 128)
            sb_ref[0, 0, bl, :] = st_scr[1, :, bl].T


def _ssd_call(xs, bm, cm, dt, z, lw, l, init, state_bufs):
    bsz, n, _ = xs.shape
    nc = n // SSD_CHUNK
    seq = lambda w: pl.BlockSpec((1, n, w), lambda b: (b, 0, 0))
    st = pl.BlockSpec((1, 1, SSD_INNER, SSD_STATE), lambda b: (b, l, 0, 0))
    layer = lambda shp: pl.BlockSpec((1,) + shp, lambda b: (l,) + (0,) * len(shp))
    in_specs = [seq(512), seq(256), seq(256), seq(128), seq(512),
                layer((8, 128)), layer((1, 512)), layer((1, 512)),
                pl.BlockSpec((256, 3072), lambda b: (0, 0))]
    args = [xs, bm, cm, dt, z, lw["prm"], lw["dexp"], lw["ssd_nw"], lw["emat"]]
    has_init = init is not None
    emit_state = state_bufs is not None
    if has_init:
        in_specs += [st, st]
        args += list(init)
    out_shape = [jax.ShapeDtypeStruct((bsz, n, SSD_INNER), F32)]
    out_specs = [seq(512)]
    aliases = {}
    state_alias = emit_state and not isinstance(state_bufs[0], jax.ShapeDtypeStruct)
    if emit_state:
        if state_alias:
            aliases = {len(args): 1, len(args) + 1: 2}
            in_specs += [pl.BlockSpec(memory_space=pl.ANY)] * 2
            args += list(state_bufs)
        out_shape += [jax.ShapeDtypeStruct(state_bufs[0].shape, F32)] * 2
        out_specs += [st, st]
    return pl.pallas_call(
        functools.partial(_ssd_kernel, nc=nc, has_init=has_init, emit_state=emit_state,
                          state_alias=state_alias),
        grid=(bsz,),
        in_specs=in_specs,
        out_specs=tuple(out_specs),
        out_shape=tuple(out_shape),
        input_output_aliases=aliases,
        scratch_shapes=[pltpu.VMEM((2, SSD_STATE, SSD_INNER), F32)]
                       + [pltpu.VMEM((n, SSD_INNER), F32)] * 2
                       + [pltpu.VMEM((n, SSD_INNER), BF16)] * 4
                       + [pltpu.VMEM((n, 128), F32)],
        compiler_params=_params("arbitrary"),
        name="ssd_scan_init" if has_init else "ssd_scan",
    )(*args)


def _attn_kernel(*refs, lam_init, has_prefix, from_cache):
    refs = list(refs)
    q_ref, k_ref, v_ref = refs[:3]
    pos = 3
    if has_prefix:
        kp_ref, vp_ref = refs[pos:pos + 2]
        pos += 2
    lp_ref, nw_ref, ones_ref, o_ref = refs[pos:pos + 4]

    q = q_ref[0]
    tq = q.shape[0]
    if from_cache:
        segs = [(k_ref[0, 0].astype(BF16), v_ref[0, 0].T.astype(BF16))]
    else:
        segs = [(k_ref[0], v_ref[0])]
    if has_prefix:
        segs.append((kp_ref[0, 0].astype(BF16), vp_ref[0, 0].T.astype(BF16)))
    lp = lp_ref[0]
    lam = (jnp.exp(jnp.sum(lp[0:1] * lp[1:2], axis=-1, keepdims=True))
           - jnp.exp(jnp.sum(lp[2:3] * lp[3:4], axis=-1, keepdims=True)) + lam_init)
    lane = lax.broadcasted_iota(jnp.int32, (1, 128), 1)
    row = lax.broadcasted_iota(jnp.int32, (128, 1), 0)

    def scores(head):
        g, hh = divmod(head, 2)
        gs = slice(g * 128, (g + 1) * 128)
        qg = q[:, gs]
        qms = []
        for s in range(2):
            lo = (2 * hh + s) * DIFF_HD
            qms.append(jnp.where((lane >= lo) & (lane < lo + DIFF_HD), qg, jnp.zeros_like(qg)))
        qm = jnp.concatenate(qms, axis=0)
        return [lax.dot_general(kk[:, gs], qm, _NT, preferred_element_type=F32) for kk, _ in segs]

    def weighted_values(head, scs):
        g, hh = divmod(head, 2)
        gs = slice(g * 128, (g + 1) * 128)
        own = slice(DIFF_VD * hh, DIFF_VD * (hh + 1))
        other = DIFF_VD * (1 - hh)
        vmod = []
        for _, vt in segs:
            mine = vt[g * 128 + DIFF_VD * hh:g * 128 + DIFF_VD * (hh + 1), :]
            ones = jnp.ones_like(mine)
            vmod.append(jnp.concatenate([mine, ones] if hh == 0 else [ones, mine], axis=0))
        mx = jnp.max(scs[0], axis=0, keepdims=True)
        for sc in scs[1:]:
            mx = jnp.maximum(mx, jnp.max(sc, axis=0, keepdims=True))
        res = None
        for sc, vm in zip(scs, vmod):
            part = jnp.dot(vm, jnp.exp2(sc - mx).astype(BF16), preferred_element_type=F32)
            res = part if res is None else res + part
        r0, r1 = res[:, :tq], res[:, tq:]
        return r0[own] * (1.0 / r0[other:other + 1]) - r1[own] * (lam / r1[other:other + 1])

    o_heads = []
    pending = scores(0)
    for head in range(DIFF_HEADS):
        nxt = scores(head + 1) if head + 1 < DIFF_HEADS else None
        o_heads.append(weighted_values(head, pending))
        pending = nxt
    o = jnp.concatenate(o_heads, axis=0).T
    ms = jnp.dot(jnp.concatenate(_split2(o * o), axis=1), ones_ref[...],
                 preferred_element_type=F32) * (1.0 / DIFF_VD)
    o_ref[0] = o * lax.rsqrt(ms + RMS_EPS) * nw_ref[0] * (1.0 - lam_init)


def _head_ones():
    e = np.zeros((512, 256), np.float32)
    for part in range(2):
        for h in range(DIFF_HEADS):
            e[part * 256 + h * 64:part * 256 + (h + 1) * 64, h * 64:(h + 1) * 64] = 1.0
    return jnp.asarray(e, dtype=BF16)


def _attn_call(q, k, v, prefix, lw, l, lam_init, from_cache, tq=256):
    bsz, n, _ = q.shape
    layer = lambda shp: pl.BlockSpec((1,) + shp, lambda b, i: (l,) + (0,) * len(shp))
    if from_cache:
        kv_spec = pl.BlockSpec((1, 1, n, DIFF_WIDTH), lambda b, i: (b, l, 0, 0))
    else:
        kv_spec = pl.BlockSpec((1, n, DIFF_WIDTH), lambda b, i: (b, 0, 0))
    v_spec = kv_spec if from_cache else pl.BlockSpec((1, DIFF_WIDTH, n), lambda b, i: (b, 0, 0))
    in_specs = [pl.BlockSpec((1, tq, DIFF_WIDTH), lambda b, i: (b, i, 0)), kv_spec, v_spec]
    args = [q, k, v]
    has_prefix = prefix is not None
    if has_prefix:
        past = prefix[0].shape[2]
        in_specs += [pl.BlockSpec((1, 1, past, DIFF_WIDTH), lambda b, i: (b, l, 0, 0))] * 2
        args += list(prefix)
    in_specs += [layer((4, DIFF_HD)), layer((1, DIFF_WIDTH)), pl.BlockSpec((512, 256), lambda b, i: (0, 0))]
    args += [lw["lp"], lw["attn_nw"], lw["ones"]]
    return pl.pallas_call(
        functools.partial(_attn_kernel, lam_init=lam_init, has_prefix=has_prefix, from_cache=from_cache),
        grid=(bsz, n // tq),
        in_specs=in_specs,
        out_specs=pl.BlockSpec((1, tq, DIFF_WIDTH), lambda b, i: (b, i, 0)),
        out_shape=jax.ShapeDtypeStruct((bsz, n, DIFF_WIDTH), F32),
        compiler_params=_params("arbitrary", "arbitrary"),
        name="diff_attn_prefix" if has_prefix else "diff_attn",
    )(*args)


def _mlp_kernel(x_ref, yssd_ref, ysc_ref, yat_ref, mod_ref, wo_ref, wu_ref, wd_ref, ln_ref, o_ref, *, parts):
    g1 = mod_ref[0, 2:3, :]
    sh2 = mod_ref[0, 3:4, :]
    sc2 = mod_ref[0, 4:5, :]
    g2 = mod_ref[0, 5:6, :]
    rows_per = x_ref.shape[1] // parts

    def mix_norm(r):
        rows = slice(r * rows_per, (r + 1) * rows_per)
        m = (jnp.dot(yssd_ref[0, rows, :].astype(BF16), wo_ref[0, 0:512, :], preferred_element_type=F32)
             + jnp.dot(ysc_ref[0, rows, :].astype(BF16), wo_ref[0, 512:768, :], preferred_element_type=F32)
             + jnp.dot(yat_ref[0, rows, :].astype(BF16), wo_ref[0, 768:1024, :], preferred_element_type=F32))
        x1 = _layer_norm(ALPHA * x_ref[0, rows, :] + g1 * m, ln_ref[0, 0:1, :], ln_ref[0, 1:2, :])
        return x1, (x1 * (1.0 + sc2) + sh2).astype(BF16)

    def mlp_norm(r, x1, h):
        rows = slice(r * rows_per, (r + 1) * rows_per)
        up = jnp.dot(h, wu_ref[0], preferred_element_type=F32)
        act = jnp.square(jnp.maximum(up, 0.0)).astype(BF16)
        f = jnp.dot(act, wd_ref[0], preferred_element_type=F32)
        o_ref[0, rows, :] = _layer_norm(ALPHA * x1 + g2 * f, ln_ref[0, 2:3, :], ln_ref[0, 3:4, :])

    staged = mix_norm(0)
    for r in range(parts):
        nxt = mix_norm(r + 1) if r + 1 < parts else None
        mlp_norm(r, *staged)
        staged = nxt


def _mlp_call(x, yssd, ysc, yat, mod, lw, l, tm):
    bsz, n, _ = x.shape
    bm = mod.shape[0]
    mod_idx = (lambda b, i: (b, 0, 0)) if bm > 1 else (lambda b, i: (0, 0, 0))
    tok = lambda w: pl.BlockSpec((1, tm, w), lambda b, i: (b, i, 0))
    layer = lambda shp: pl.BlockSpec((1,) + shp, lambda b, i: (l,) + (0,) * len(shp),
                                     pipeline_mode=pl.Buffered(1))
    return pl.pallas_call(
        functools.partial(_mlp_kernel, parts=tm // 256),
        grid=(bsz, n // tm),
        in_specs=[tok(D_MODEL), tok(512), tok(256), tok(256),
                  pl.BlockSpec((1, 6, D_MODEL), mod_idx),
                  layer((D_MODEL, D_MODEL)), layer((D_MODEL, D_FF)), layer((D_FF, D_MODEL)),
                  layer((4, D_MODEL))],
        out_specs=tok(D_MODEL),
        out_shape=jax.ShapeDtypeStruct((bsz, n, D_MODEL), F32),
        compiler_params=_params("arbitrary", "arbitrary"),
        name="out_mlp",
    )(x, yssd, ysc, yat, mod, lw["w_out"], lw["w_up"], lw["w_down"], lw["ln"])


def _rope_tables(n):
    rows = n // GRID_W
    row = jnp.repeat(jnp.arange(rows, dtype=F32), GRID_W)
    col = jnp.tile(jnp.arange(GRID_W, dtype=F32), rows)
    inv = ROPE_BASE ** (-jnp.arange(ROPE_F, dtype=F32) / ROPE_F)
    ang = jnp.stack([row[:, None] * inv, col[:, None] * inv], axis=1)
    ang = jnp.repeat(ang[:, :, None, :], 2, axis=2).reshape(n, DIFF_HD)
    sign = jnp.tile(jnp.concatenate([-jnp.ones(ROPE_F, F32), jnp.ones(ROPE_F, F32)]), 2)
    cos_t = jnp.tile(jnp.cos(ang), (1, 2 * DIFF_HEADS))
    sin_t = jnp.tile(jnp.sin(ang) * sign, (1, 2 * DIFF_HEADS))
    return cos_t, sin_t


def _layer_weights(w_in, ssd_conv_w, ssd_conv_b, ssd_dt_bias, ssd_a_log, ssd_d, ssd_norm_w, sc_conv_w,
                   diff_lambda, diff_norm_w, w_out, ln1_g, ln1_b, w_up, w_down, ln2_g, ln2_b):
    w_in_p = _permute_w_in(w_in)
    prm = jnp.zeros((DEPTH, 8, 128), F32)
    prm = prm.at[:, 0, :16].set(ssd_dt_bias.reshape(DEPTH, 16)).at[:, 1, :16].set(ssd_a_log.reshape(DEPTH, 16))
    return {
        "w_in": w_in_p,
        "w_vt": jnp.swapaxes(w_in[:, :, IN_COLS - DIFF_WIDTH:], 1, 2).astype(BF16),
        "conv_w": ssd_conv_w,
        "conv_b": ssd_conv_b.reshape(DEPTH, 1, SSD_CONV_CH),
        "sc_w": sc_conv_w,
        "prm": prm,
        "dexp": jnp.repeat(ssd_d, SSD_HEAD_DIM, axis=1).reshape(DEPTH, 1, SSD_INNER),
        "ssd_nw": ssd_norm_w.reshape(DEPTH, 1, SSD_INNER),
        "emat": _expand_matrix(),
        "lp": diff_lambda,
        "attn_nw": jnp.tile(diff_norm_w, (1, DIFF_HEADS)).reshape(DEPTH, 1, DIFF_WIDTH),
        "ones": _head_ones(),
        "w_out": w_out.astype(BF16),
        "w_up": w_up.astype(BF16),
        "w_down": w_down.astype(BF16),
        "ln": jnp.stack([ln1_g, ln1_b, ln2_g, ln2_b], axis=1),
    }


def _trunk_layer(x, mod, lw, l, lam_init, rope, prefix, init, caches, state_bufs, tm):
    bsz, n, _ = x.shape
    res = _inproj_call(x, mod, lw, l, rope, caches, tm)
    z, xs, bm, cm, dt, ysc, q, k, v = res
    ssd = _ssd_call(xs, bm, cm, dt, z, lw, l, init, state_bufs)
    yat = _attn_call(q, k, v, prefix, lw, l, lam_init, from_cache=caches is not None, tq=min(n, 512))
    if mod.shape[0] == 1:
        flat = lambda t: t.reshape(1, bsz * n, t.shape[-1])
        y = _mlp_call(flat(x), flat(ssd[0]), flat(ysc), flat(yat), mod, lw, l, 512).reshape(bsz, n, D_MODEL)
    else:
        y = _mlp_call(x, ssd[0], ysc, yat, mod, lw, l, 1024)
    new_caches = (k, v) if caches is not None else None
    new_states = (ssd[1], ssd[2]) if state_bufs is not None else None
    return y, new_caches, new_states


def kernel(x_prompt, x_sample, cache_k, cache_v, state_ssm_fwd, state_ssm_bwd, c, c_ctx, w_mod, b_mod, w_in,
           ssd_conv_w, ssd_conv_b, ssd_dt_bias, ssd_a_log, ssd_d, ssd_norm_w, sc_conv_w, diff_lambda,
           diff_norm_w, w_out, ln1_g, ln1_b, w_up, w_down, ln2_g, ln2_b):
    bp, seq, _ = x_prompt.shape
    bs, dec_seq, _ = x_sample.shape
    past = cache_k.shape[2]

    cvec = jnp.zeros((16, D_MODEL), F32).at[:bs].set(c).at[bs].set(c_ctx)
    mods = _mod_call(cvec, w_mod, b_mod)
    rope = _rope_tables(dec_seq)
    lw = _layer_weights(w_in, ssd_conv_w, ssd_conv_b, ssd_dt_bias, ssd_a_log, ssd_d, ssd_norm_w, sc_conv_w,
                        diff_lambda, diff_norm_w, w_out, ln1_g, ln1_b, w_up, w_down, ln2_g, ln2_b)
    prefix = (cache_k.reshape(bs, DEPTH, past, DIFF_WIDTH), cache_v.reshape(bs, DEPTH, past, DIFF_WIDTH))
    init = (state_ssm_fwd.reshape(bs, DEPTH, SSD_INNER, SSD_STATE),
            state_ssm_bwd.reshape(bs, DEPTH, SSD_INNER, SSD_STATE))
    caches = (jax.ShapeDtypeStruct((bp, DEPTH, seq, DIFF_WIDTH), F32),) * 2
    states = (jax.ShapeDtypeStruct((bp, DEPTH, SSD_INNER, SSD_STATE), F32),) * 2

    hp, hs = x_prompt, x_sample
    for l in range(DEPTH):
        lam_init = 0.8 - 0.6 * math.exp(-0.3 * l)
        mod_ctx = mods[l, bs:bs + 1].reshape(1, 6, D_MODEL)
        mod_lat = mods[l, :bs].reshape(bs, 6, D_MODEL)
        hp, caches, states = _trunk_layer(hp, mod_ctx, lw, l, lam_init, None, None, None, caches, states, seq)
        hs = _trunk_layer(hs, mod_lat, lw, l, lam_init, rope, prefix, init, None, None, 1024)[0]

    return (hp, hs,
            caches[0].reshape(bp, DEPTH, seq, 2 * DIFF_HEADS, DIFF_HD),
            caches[1].reshape(bp, DEPTH, seq, DIFF_HEADS, DIFF_VD),
            states[0].reshape(bp, DEPTH, SSD_HEADS, SSD_HEAD_DIM, SSD_STATE),
            states[1].reshape(bp, DEPTH, SSD_HEADS, SSD_HEAD_DIM, SSD_STATE))
```

```python
import functools
import math

import jax
import jax.numpy as jnp
import numpy as np
from jax import lax
from jax.experimental import pallas as pl
from jax.experimental.pallas import tpu as pltpu

F32 = jnp.float32
BF16 = jnp.bfloat16

D_MODEL = 1024
DEPTH = 4
GRID_W = 64
SSD_INNER = 512
SSD_HEAD_DIM = 64
SSD_HEADS = 8
SSD_STATE = 128
SSD_CHUNK = 128
SSD_CONV = 5
SSD_CONV_CH = 1024
SC_WIDTH = 256
SC_CONV = 3
DIFF_WIDTH = 256
DIFF_HEADS = 4
DIFF_HD = 32
DIFF_VD = 64
ROPE_BASE = 10000.0
ROPE_F = 8
D_FF = 4096
ALPHA = (2 * DEPTH) ** 0.25
LN_EPS = 1e-5
RMS_EPS = 1e-6

C_Z = 0
C_XBC = 512
C_SCB = 1536
C_SCC = 1792
C_Q = 2304
C_K = 2560
C_V = 2816
C_DT = 3072
IN_COLS = 3088
IN_COLS_PAD = 3200

HALO = 8
VMEM_LIMIT = 56 * 1024 * 1024
NEG_BIG = -1e30
Q_PRESCALE = DIFF_HD ** -0.5 * math.log2(math.e)

_NT = (((1,), (1,)), ((), ()))
_TN = (((0,), (0,)), ((), ()))


def _split2(x):
    hi = x.astype(BF16)
    lo = (x - hi.astype(F32)).astype(BF16)
    return hi, lo


def _split3(x):
    hi = x.astype(BF16)
    r1 = x - hi.astype(F32)
    mid = r1.astype(BF16)
    lo = (r1 - mid.astype(F32)).astype(BF16)
    return hi, mid, lo


def _silu(x):
    return x * jax.nn.sigmoid(x)


def _layer_norm(x, g, b):
    mu = jnp.mean(x, axis=-1, keepdims=True)
    xc = x - mu
    var = jnp.mean(xc * xc, axis=-1, keepdims=True)
    return xc * lax.rsqrt(var + LN_EPS) * g + b


def _params(*sem):
    return pltpu.CompilerParams(dimension_semantics=sem, vmem_limit_bytes=VMEM_LIMIT)


def _mod_kernel(c_ref, w_ref, b_ref, o_ref):
    a = _silu(c_ref[...]).astype(BF16)
    o_ref[0] = jnp.dot(a, w_ref[0].astype(BF16), preferred_element_type=F32) + b_ref[0]


def _mod_call(cvec, w_mod, b_mod):
    tn = 1536
    return pl.pallas_call(
        _mod_kernel,
        grid=(DEPTH, 6 * D_MODEL // tn),
        in_specs=[
            pl.BlockSpec((16, D_MODEL), lambda l, j: (0, 0)),
            pl.BlockSpec((1, D_MODEL, tn), lambda l, j: (l, 0, j)),
            pl.BlockSpec((1, 1, tn), lambda l, j: (l, 0, j)),
        ],
        out_specs=pl.BlockSpec((1, 16, tn), lambda l, j: (l, 0, j)),
        out_shape=jax.ShapeDtypeStruct((DEPTH, 16, 6 * D_MODEL), F32),
        compiler_params=_params("arbitrary", "arbitrary"),
        name="mod_vectors",
    )(cvec, w_mod, b_mod.reshape(DEPTH, 1, 6 * D_MODEL))


def _permute_kernel(w_ref, o_ref):
    cols = w_ref.shape[-1]
    o_ref[0, 0:C_SCB, :] = w_ref[0, 0:C_SCB, :].astype(BF16)
    o_ref[0, C_SCB:C_DT, :] = w_ref[0, C_SCB + 16:IN_COLS, :].astype(BF16)
    o_ref[0, C_DT:C_DT + 16, :] = w_ref[0, C_SCB:C_SCB + 16, :].astype(BF16)
    o_ref[0, C_DT + 16:IN_COLS_PAD, :] = jnp.zeros((IN_COLS_PAD - IN_COLS, cols), BF16)


def _permute_w_in(w_in_t):
    tc = 256
    return pl.pallas_call(
        _permute_kernel,
        grid=(DEPTH, D_MODEL // tc),
        in_specs=[pl.BlockSpec((1, IN_COLS, tc), lambda l, i: (l, 0, i))],
        out_specs=pl.BlockSpec((1, IN_COLS_PAD, tc), lambda l, i: (l, 0, i)),
        out_shape=jax.ShapeDtypeStruct((DEPTH, IN_COLS_PAD, D_MODEL), BF16),
        compiler_params=_params("arbitrary", "arbitrary"),
        name="permute_w_in",
    )(w_in_t)


def _inproj_kernel(*refs, tm, n_tiles, use_rope, cache_out, cache_alias):
    refs = list(refs)
    x_ref, xp_ref, xn_ref, mod_ref, w_ref, cw_ref, cb_ref, scw_ref = refs[:8]
    pos = 8
    if use_rope:
        cos_ref, sin_ref = refs[pos:pos + 2]
        pos += 2
    if cache_alias:
        pos += 2
    z_ref, xs_ref, b_ref, c_ref, dt_ref, ysc_ref, q_ref, k_ref, v_ref = refs[pos:pos + 9]
    i = pl.program_id(1)
    sh1 = mod_ref[0, 0:1, :]
    one_sc1 = 1.0 + mod_ref[0, 1:2, :]
    rows_e = tm + 2 * HALO
    xe = jnp.concatenate([xp_ref[0], x_ref[0], xn_ref[0]], axis=0)
    ue = (xe * one_sc1 + sh1).astype(BF16)
    u = (x_ref[0] * one_sc1 + sh1).astype(BF16)
    rowid = lax.broadcasted_iota(jnp.int32, (rows_e, 1), 0)
    keep = jnp.where(rowid < HALO, jnp.where(i > 0, 1.0, 0.0),
                     jnp.where(rowid >= tm + HALO, jnp.where(i < n_tiles - 1, 1.0, 0.0), 1.0))

    def mm(a, lo, hi):
        return lax.dot_general(a, w_ref[0, lo:hi, :], _NT, preferred_element_type=F32)

    def shifted(t, off):
        r = t if off == 0 else pltpu.roll(t, (rows_e - off) % rows_e, 0)
        return r[HALO:HALO + tm]

    z_ref[0] = mm(u, C_Z, C_XBC)
    dt_ref[0] = mm(u, C_DT, IN_COLS_PAD)

    xbc = mm(ue, C_XBC, C_SCB) * keep
    acc = cw_ref[0, 0:1, :] * shifted(xbc, -2) + cb_ref[0]
    for kk in range(1, SSD_CONV):
        acc = acc + cw_ref[0, kk:kk + 1, :] * shifted(xbc, kk - 2)
    act = _silu(acc)
    xs_ref[0] = act[:, :SSD_INNER]
    b_ref[0] = act[:, SSD_INNER:SSD_INNER + 256].astype(b_ref.dtype)
    c_ref[0] = act[:, SSD_INNER + 256:].astype(c_ref.dtype)

    scb = mm(u, C_SCB, C_SCC)
    ch = mm(ue, C_SCC, C_Q)
    prod = ch[:, :SC_WIDTH] * ch[:, SC_WIDTH:] * keep
    conv = scw_ref[0, 0:1, :] * shifted(prod, -1)
    for kk in range(1, SC_CONV):
        conv = conv + scw_ref[0, kk:kk + 1, :] * shifted(prod, kk - 1)
    ysc_ref[0] = scb * conv

    q = mm(u, C_Q, C_K)
    k = mm(u, C_K, C_V)
    if use_rope:
        cos = cos_ref[...]
        sin = sin_ref[...]
        lane = lax.broadcasted_iota(jnp.int32, (tm, DIFF_WIDTH), 1)
        first = (lane & 15) < 8

        def rope(t):
            sw = jnp.where(first, pltpu.roll(t, DIFF_WIDTH - 8, 1), pltpu.roll(t, 8, 1))
            return t * cos + sw * sin

        q = rope(q)
        k = rope(k)
    q_ref[0] = (q * Q_PRESCALE).astype(q_ref.dtype)
    if cache_out:
        k_ref[0, 0] = k
        v_ref[0, 0] = mm(u, C_V, C_DT)
    else:
        k_ref[0] = k.astype(k_ref.dtype)
        v_ref[0] = lax.dot_general(w_ref[0, C_V:C_DT, :], u, _NT, preferred_element_type=F32).astype(v_ref.dtype)


def _inproj_call(x, mod, lw, l, rope, caches, tm):
    bsz, n, _ = x.shape
    n_tiles = n // tm
    tpb = tm // 8
    bm = mod.shape[0]
    mod_idx = (lambda b, i: (b, 0, 0)) if bm > 1 else (lambda b, i: (0, 0, 0))
    tok = lambda w: pl.BlockSpec((1, tm, w), lambda b, i: (b, i, 0))
    layer = lambda shp: pl.BlockSpec((1,) + shp, lambda b, i: (l,) + (0,) * len(shp))
    in_specs = [
        tok(D_MODEL),
        pl.BlockSpec((1, HALO, D_MODEL), lambda b, i: (b, jnp.maximum(i * tpb - 1, 0), 0)),
        pl.BlockSpec((1, HALO, D_MODEL), lambda b, i: (b, jnp.minimum((i + 1) * tpb, n // 8 - 1), 0)),
        pl.BlockSpec((1, 6, D_MODEL), mod_idx),
        layer((IN_COLS_PAD, D_MODEL)),
        layer((SSD_CONV, SSD_CONV_CH)),
        layer((1, SSD_CONV_CH)),
        layer((SC_CONV, SC_WIDTH)),
    ]
    args = [x, x, x, mod, lw["w_in"], lw["conv_w"], lw["conv_b"], lw["sc_w"]]
    use_rope = rope is not None
    if use_rope:
        in_specs += [pl.BlockSpec((tm, DIFF_WIDTH), lambda b, i: (i, 0))] * 2
        args += list(rope)
    sds = lambda w, dt: jax.ShapeDtypeStruct((bsz, n, w), dt)
    out_shape = [sds(512, F32), sds(512, F32), sds(256, BF16), sds(256, BF16), sds(128, F32),
                 sds(256, F32), sds(256, BF16)]
    out_specs = [tok(s.shape[-1]) for s in out_shape]
    aliases = {}
    cache_out = caches is not None
    cache_alias = cache_out and not isinstance(caches[0], jax.ShapeDtypeStruct)
    if cache_out:
        if cache_alias:
            aliases = {len(args): 7, len(args) + 1: 8}
            in_specs += [pl.BlockSpec(memory_space=pl.ANY)] * 2
            args += list(caches)
        out_shape += [jax.ShapeDtypeStruct(caches[0].shape, F32)] * 2
        out_specs += [pl.BlockSpec((1, 1, tm, DIFF_WIDTH), lambda b, i: (b, l, i, 0))] * 2
    else:
        out_shape += [sds(256, BF16), jax.ShapeDtypeStruct((bsz, DIFF_WIDTH, n), BF16)]
        out_specs += [tok(256), pl.BlockSpec((1, DIFF_WIDTH, tm), lambda b, i: (b, 0, i))]
    return pl.pallas_call(
        functools.partial(_inproj_kernel, tm=tm, n_tiles=n_tiles, use_rope=use_rope, cache_out=cache_out,
                          cache_alias=cache_alias),
        grid=(bsz, n_tiles),
        in_specs=in_specs,
        out_specs=tuple(out_specs),
        out_shape=tuple(out_shape),
        input_output_aliases=aliases,
        compiler_params=_params("arbitrary", "arbitrary"),
        name="in_proj_rope" if use_rope else "in_proj",
    )(*args)


def _expand_matrix():
    e = np.zeros((256, 3 * 1024), np.float32)
    for part in range(2):
        for qn in range(3):
            for d in range(2):
                for h in range(SSD_HEADS):
                    r = part * 128 + qn * 16 + d * 8 + h
                    c0 = qn * 1024 + d * 512 + h * 64
                    e[r, c0:c0 + 64] = 1.0
    return jnp.asarray(e, dtype=BF16)


def _ssd_kernel(*refs, nc, has_init, emit_state, state_alias):
    refs = list(refs)
    xs_ref, b_ref, c_ref, dt_ref, z_ref, prm_ref, dexp_ref, nw_ref, emat_ref = refs[:9]
    pos = 9
    if has_init:
        s0f_ref, s0b_ref = refs[pos:pos + 2]
        pos += 2
    if state_alias:
        pos += 2
    y_ref = refs[pos]
    pos += 1
    if emit_state:
        sf_ref, sb_ref = refs[pos:pos + 2]
        pos += 2
    st_scr, gxf_scr, gxb_scr, xdtf_scr, xdtb_scr, xdf_scr, xdb_scr, acum_scr = refs[pos:pos + 8]
    grp = 4 if nc % 4 == 0 else (2 if nc % 2 == 0 else 1)

    lane = lax.broadcasted_iota(jnp.int32, (1, 128), 1)
    ii = lax.broadcasted_iota(jnp.int32, (128, 128), 0)
    jj = lax.broadcasted_iota(jnp.int32, (128, 128), 1)
    low = jj <= ii
    upp = jj >= ii
    ltri = jnp.where(low, 1.0, 0.0).astype(BF16)
    bias = prm_ref[0, 0:1, :]
    a_row = -jnp.exp(prm_ref[0, 1:2, :])
    is_head = lane < 2 * SSD_HEADS
    is_fwd = lane < SSD_HEADS
    half = lane < 64

    for blk in range(4):
        bl = slice(blk * 128, (blk + 1) * 128)
        if has_init:
            st_scr[0, :, bl] = s0f_ref[0, 0, bl, :].T
            st_scr[1, :, bl] = s0b_ref[0, 0, bl, :].T
        else:
            st_scr[0, :, bl] = jnp.zeros((SSD_STATE, 128), F32)
            st_scr[1, :, bl] = jnp.zeros((SSD_STATE, 128), F32)

    def prep_body(t, carry):
        r0 = pl.multiple_of(t * (grp * SSD_CHUNK), grp * SSD_CHUNK)
        rows = pl.ds(r0, grp * SSD_CHUNK)
        dtv = jnp.where(is_head, jax.nn.softplus(dt_ref[0, rows, :] + bias), 0.0)
        dta = dtv * a_row
        chunk = lambda v, c: v[c * SSD_CHUNK:(c + 1) * SSD_CHUNK]
        parts = []
        for c in range(grp):
            parts += list(_split3(chunk(dta, c)))
        a3 = jnp.dot(ltri, jnp.concatenate(parts, axis=1), preferred_element_type=F32)
        acums, packs = [], []
        for c in range(grp):
            o = c * 384
            cum = a3[:, o:o + 128] + a3[:, o + 128:o + 256] + a3[:, o + 256:o + 384]
            total = cum[127:128, :]
            acum = jnp.where(is_fwd, cum, total - cum + chunk(dta, c))
            gdec = jnp.where(is_head, jnp.exp(acum), 0.0)
            wdec = chunk(dtv, c) * jnp.exp(total - acum)
            acums.append(acum)
            packs.append(chunk(dtv, c) + pltpu.roll(wdec, 16, 1) + pltpu.roll(gdec, 32, 1))
        packed = jnp.concatenate(packs, axis=0)
        x6 = jnp.dot(jnp.concatenate(_split2(packed), axis=1), emat_ref[...], preferred_element_type=F32)
        x = xs_ref[0, rows, :]
        xdtf_scr[rows, :] = (x * x6[:, 0:512]).astype(BF16)
        xdtb_scr[rows, :] = (x * x6[:, 512:1024]).astype(BF16)
        xdf_scr[rows, :] = (x * x6[:, 1024:1536]).astype(BF16)
        xdb_scr[rows, :] = (x * x6[:, 1536:2048]).astype(BF16)
        gxf_scr[rows, :] = x6[:, 2048:2560]
        gxb_scr[rows, :] = x6[:, 2560:3072]
        acum_scr[rows, :] = jnp.concatenate(acums, axis=0)
        return carry

    lax.fori_loop(0, nc // grp, prep_body, 0)

    def fwd_body(c, carry):
        r0 = pl.multiple_of(c * SSD_CHUNK, SSD_CHUNK)
        rows = pl.ds(r0, SSD_CHUNK)
        acum = acum_scr[rows, :]
        acum_t = acum.T
        gx_f = gxf_scr[rows, :]
        xdt_f = xdtf_scr[rows, :]
        xdt_b = xdtb_scr[rows, :]
        xd_f = xdf_scr[rows, :]
        bc = b_ref[0, rows, :]
        cc = c_ref[0, rows, :]

        y_parts = []
        for g in range(2):
            gl = slice(g * 256, (g + 1) * 256)
            bg = bc[:, g * 128:(g + 1) * 128]
            cg = cc[:, g * 128:(g + 1) * 128]
            cb = lax.dot_general(cg, bg, _NT, preferred_element_type=F32)
            s_f = st_scr[0, :, gl]
            yo_f = jnp.dot(cg, s_f.astype(BF16), preferred_element_type=F32) * gx_f[:, gl]
            st_scr[0, :, gl] = s_f * gx_f[127:128, gl] + lax.dot_general(
                bg, xd_f[:, gl], _TN, preferred_element_type=F32)
            for pair in range(2):
                hp = 2 * g + pair
                pl_ = slice(hp * 128, (hp + 1) * 128)
                rhs = jnp.concatenate([xdt_f[:, pl_], xdt_b[:, pl_]], axis=0)
                ys = []
                for sub in range(2):
                    h = 2 * hp + sub
                    hb = SSD_HEADS + h
                    seg_f = acum[:, h:h + 1] - acum_t[h:h + 1, :]
                    m_f = jnp.exp(jnp.where(low, seg_f, NEG_BIG)) * cb
                    seg_b = acum[:, hb:hb + 1] - acum_t[hb:hb + 1, :]
                    m_b = jnp.exp(jnp.where(upp, seg_b, NEG_BIG)) * cb
                    lhs = jnp.concatenate([m_f, m_b], axis=1).astype(BF16)
                    ys.append(jnp.dot(lhs, rhs, preferred_element_type=F32))
                y_pair = jnp.where(half, ys[0], ys[1])
                y_parts.append(y_pair + yo_f[:, pair * 128:(pair + 1) * 128])
        y_ref[0, rows, :] = jnp.concatenate(y_parts, axis=1)
        return carry

    lax.fori_loop(0, nc, fwd_body, 0, unroll=2)
    if emit_state:
        for blk in range(4):
            bl = slice(blk * 128, (blk + 1) * 128)
            sf_ref[0, 0, bl, :] = st_scr[0, :, bl].T

    dexp = dexp_ref[0]
    nw = nw_ref[0]

    def bwd_body(t, carry):
        c = nc - 1 - t
        r0 = pl.multiple_of(c * SSD_CHUNK, SSD_CHUNK)
        rows = pl.ds(r0, SSD_CHUNK)
        bc = b_ref[0, rows, :]
        cc = c_ref[0, rows, :]
        gx_b = gxb_scr[rows, :]
        xd_b = xdb_scr[rows, :]
        yo = []
        for g in range(2):
            gl = slice(g * 256, (g + 1) * 256)
            bg = bc[:, g * 128:(g + 1) * 128]
            cg = cc[:, g * 128:(g + 1) * 128]
            s_b = st_scr[1, :, gl]
            yo.append(jnp.dot(cg, s_b.astype(BF16), preferred_element_type=F32) * gx_b[:, gl])
            st_scr[1, :, gl] = s_b * gx_b[0:1, gl] + lax.dot_general(
                bg, xd_b[:, gl], _TN, preferred_element_type=F32)
        x = xs_ref[0, rows, :]
        y = y_ref[0, rows, :] + jnp.concatenate(yo, axis=1) + dexp * x
        y = y * _silu(z_ref[0, rows, :])
        ms = jnp.mean(y * y, axis=-1, keepdims=True)
        y_ref[0, rows, :] = y * lax.rsqrt(ms + RMS_EPS) * nw
        return carry

    lax.fori_loop(0, nc, bwd_body, 0, unroll=2)
    if emit_state:
        for blk in range(4):
            bl = slice(blk * 128, (blk + 1) * 128)
            sb_ref[0, 0, bl, :] = st_scr[1, :, bl].T


def _ssd_call(xs, bm, cm, dt, z, lw, l, init, state_bufs):
    bsz, n, _ = xs.shape
    nc = n // SSD_CHUNK
    seq = lambda w: pl.BlockSpec((1, n, w), lambda b: (b, 0, 0))
    st = pl.BlockSpec((1, 1, SSD_INNER, SSD_STATE), lambda b: (b, l, 0, 0))
    layer = lambda shp: pl.BlockSpec((1,) + shp, lambda b: (l,) + (0,) * len(shp))
    in_specs = [seq(512), seq(256), seq(256), seq(128), seq(512),
                layer((8, 128)), layer((1, 512)), layer((1, 512)),
                pl.BlockSpec((256, 3072), lambda b: (0, 0))]
    args = [xs, bm, cm, dt, z, lw["prm"], lw["dexp"], lw["ssd_nw"], lw["emat"]]
    has_init = init is not None
    emit_state = state_bufs is not None
    if has_init:
        in_specs += [st, st]
        args += list(init)
    out_shape = [jax.ShapeDtypeStruct((bsz, n, SSD_INNER), F32)]
    out_specs = [seq(512)]
    aliases = {}
    state_alias = emit_state and not isinstance(state_bufs[0], jax.ShapeDtypeStruct)
    if emit_state:
        if state_alias:
            aliases = {len(args): 1, len(args) + 1: 2}
            in_specs += [pl.BlockSpec(memory_space=pl.ANY)] * 2
            args += list(state_bufs)
        out_shape += [jax.ShapeDtypeStruct(state_bufs[0].shape, F32)] * 2
        out_specs += [st, st]
    return pl.pallas_call(
        functools.partial(_ssd_kernel, nc=nc, has_init=has_init, emit_state=emit_state,
                          state_alias=state_alias),
        grid=(bsz,),
        in_specs=in_specs,
        out_specs=tuple(out_specs),
        out_shape=tuple(out_shape),
        input_output_aliases=aliases,
        scratch_shapes=[pltpu.VMEM((2, SSD_STATE, SSD_INNER), F32)]
                       + [pltpu.VMEM((n, SSD_INNER), F32)] * 2
                       + [pltpu.VMEM((n, SSD_INNER), BF16)] * 4
                       + [pltpu.VMEM((n, 128), F32)],
        compiler_params=_params("arbitrary"),
        name="ssd_scan_init" if has_init else "ssd_scan",
    )(*args)


def _attn_kernel(*refs, lam_init, has_prefix, from_cache):
    refs = list(refs)
    q_ref, k_ref, v_ref = refs[:3]
    pos = 3
    if has_prefix:
        kp_ref, vp_ref = refs[pos:pos + 2]
        pos += 2
    lp_ref, nw_ref, ones_ref, o_ref = refs[pos:pos + 4]

    q = q_ref[0]
    tq = q.shape[0]
    if from_cache:
        segs = [(k_ref[0, 0].astype(BF16), v_ref[0, 0].T.astype(BF16))]
    else:
        segs = [(k_ref[0], v_ref[0])]
    if has_prefix:
        segs.append((kp_ref[0, 0].T.astype(BF16), vp_ref[0, 0].astype(BF16)))
    lp = lp_ref[0]
    lam = (jnp.exp(jnp.sum(lp[0:1] * lp[1:2], axis=-1, keepdims=True))
           - jnp.exp(jnp.sum(lp[2:3] * lp[3:4], axis=-1, keepdims=True)) + lam_init)
    lane = lax.broadcasted_iota(jnp.int32, (1, 128), 1)
    row = lax.broadcasted_iota(jnp.int32, (128, 1), 0)

    def scores(head):
        g, hh = divmod(head, 2)
        gs = slice(g * 128, (g + 1) * 128)
        qg = q[:, gs]
        qms = []
        for s in range(2):
            lo = (2 * hh + s) * DIFF_HD
            qms.append(jnp.where((lane >= lo) & (lane < lo + DIFF_HD), qg, jnp.zeros_like(qg)))
        qm = jnp.concatenate(qms, axis=0)
        return [lax.dot_general(kk[:, gs], qm, _NT, preferred_element_type=F32) for kk, _ in segs]

    def weighted_values(head, scs):
        g, hh = divmod(head, 2)
        gs = slice(g * 128, (g + 1) * 128)
        own = slice(DIFF_VD * hh, DIFF_VD * (hh + 1))
        other = DIFF_VD * (1 - hh)
        vmod = []
        for _, vt in segs:
            mine = vt[g * 128 + DIFF_VD * hh:g * 128 + DIFF_VD * (hh + 1), :]
            ones = jnp.ones_like(mine)
            vmod.append(jnp.concatenate([mine, ones] if hh == 0 else [ones, mine], axis=0))
        mx = jnp.max(scs[0], axis=0, keepdims=True)
        for sc in scs[1:]:
            mx = jnp.maximum(mx, jnp.max(sc, axis=0, keepdims=True))
        res = None
        for sc, vm in zip(scs, vmod):
            part = jnp.dot(vm, jnp.exp2(sc - mx).astype(BF16), preferred_element_type=F32)
            res = part if res is None else res + part
        r0, r1 = res[:, :tq], res[:, tq:]
        return r0[own] * (1.0 / r0[other:other + 1]) - r1[own] * (lam / r1[other:other + 1])

    o_heads = []
    pending = scores(0)
    for head in range(DIFF_HEADS):
        nxt = scores(head + 1) if head + 1 < DIFF_HEADS else None
        o_heads.append(weighted_values(head, pending))
        pending = nxt
    o = jnp.concatenate(o_heads, axis=0).T
    ms = jnp.dot(jnp.concatenate(_split2(o * o), axis=1), ones_ref[...],
                 preferred_element_type=F32) * (1.0 / DIFF_VD)
    o_ref[0] = o * lax.rsqrt(ms + RMS_EPS) * nw_ref[0] * (1.0 - lam_init)


def _head_ones():
    e = np.zeros((512, 256), np.float32)
    for part in range(2):
        for h in range(DIFF_HEADS):
            e[part * 256 + h * 64:part * 256 + (h + 1) * 64, h * 64:(h + 1) * 64] = 1.0
    return jnp.asarray(e, dtype=BF16)


def _attn_call(q, k, v, prefix, lw, l, lam_init, from_cache, tq=256):
    bsz, n, _ = q.shape
    layer = lambda shp: pl.BlockSpec((1,) + shp, lambda b, i: (l,) + (0,) * len(shp))
    if from_cache:
        kv_spec = pl.BlockSpec((1, 1, n, DIFF_WIDTH), lambda b, i: (b, l, 0, 0))
    else:
        kv_spec = pl.BlockSpec((1, n, DIFF_WIDTH), lambda b, i: (b, 0, 0))
    v_spec = kv_spec if from_cache else pl.BlockSpec((1, DIFF_WIDTH, n), lambda b, i: (b, 0, 0))
    in_specs = [pl.BlockSpec((1, tq, DIFF_WIDTH), lambda b, i: (b, i, 0)), kv_spec, v_spec]
    args = [q, k, v]
    has_prefix = prefix is not None
    if has_prefix:
        past = prefix[0].shape[3]
        in_specs += [pl.BlockSpec((1, 1, DIFF_WIDTH, past), lambda b, i: (b, l, 0, 0))] * 2
        args += list(prefix)
    in_specs += [layer((4, DIFF_HD)), layer((1, DIFF_WIDTH)), pl.BlockSpec((512, 256), lambda b, i: (0, 0))]
    args += [lw["lp"], lw["attn_nw"], lw["ones"]]
    return pl.pallas_call(
        functools.partial(_attn_kernel, lam_init=lam_init, has_prefix=has_prefix, from_cache=from_cache),
        grid=(bsz, n // tq),
        in_specs=in_specs,
        out_specs=pl.BlockSpec((1, tq, DIFF_WIDTH), lambda b, i: (b, i, 0)),
        out_shape=jax.ShapeDtypeStruct((bsz, n, DIFF_WIDTH), F32),
        compiler_params=_params("arbitrary", "arbitrary"),
        name="diff_attn_prefix" if has_prefix else "diff_attn",
    )(*args)


def _mlp_kernel(x_ref, yssd_ref, ysc_ref, yat_ref, mod_ref, wo_ref, wu_ref, wd_ref, ln_ref, o_ref, *, parts):
    g1 = mod_ref[0, 2:3, :]
    sh2 = mod_ref[0, 3:4, :]
    sc2 = mod_ref[0, 4:5, :]
    g2 = mod_ref[0, 5:6, :]
    rows_per = x_ref.shape[1] // parts

    def mix_norm(r):
        rows = slice(r * rows_per, (r + 1) * rows_per)
        m = (jnp.dot(yssd_ref[0, rows, :].astype(BF16), wo_ref[0, 0:512, :], preferred_element_type=F32)
             + jnp.dot(ysc_ref[0, rows, :].astype(BF16), wo_ref[0, 512:768, :], preferred_element_type=F32)
             + jnp.dot(yat_ref[0, rows, :].astype(BF16), wo_ref[0, 768:1024, :], preferred_element_type=F32))
        x1 = _layer_norm(ALPHA * x_ref[0, rows, :] + g1 * m, ln_ref[0, 0:1, :], ln_ref[0, 1:2, :])
        return x1, (x1 * (1.0 + sc2) + sh2).astype(BF16)

    def mlp_norm(r, x1, h):
        rows = slice(r * rows_per, (r + 1) * rows_per)
        up = jnp.dot(h, wu_ref[0], preferred_element_type=F32)
        act = jnp.square(jnp.maximum(up, 0.0)).astype(BF16)
        f = jnp.dot(act, wd_ref[0], preferred_element_type=F32)
        o_ref[0, rows, :] = _layer_norm(ALPHA * x1 + g2 * f, ln_ref[0, 2:3, :], ln_ref[0, 3:4, :])

    staged = mix_norm(0)
    for r in range(parts):
        nxt = mix_norm(r + 1) if r + 1 < parts else None
        mlp_norm(r, *staged)
        staged = nxt


def _mlp_call(x, yssd, ysc, yat, mod, lw, l, tm):
    bsz, n, _ = x.shape
    bm = mod.shape[0]
    mod_idx = (lambda b, i: (b, 0, 0)) if bm > 1 else (lambda b, i: (0, 0, 0))
    tok = lambda w: pl.BlockSpec((1, tm, w), lambda b, i: (b, i, 0))
    layer = lambda shp: pl.BlockSpec((1,) + shp, lambda b, i: (l,) + (0,) * len(shp),
                                     pipeline_mode=pl.Buffered(1))
    return pl.pallas_call(
        functools.partial(_mlp_kernel, parts=tm // 256),
        grid=(bsz, n // tm),
        in_specs=[tok(D_MODEL), tok(512), tok(256), tok(256),
                  pl.BlockSpec((1, 6, D_MODEL), mod_idx),
                  layer((D_MODEL, D_MODEL)), layer((D_MODEL, D_FF)), layer((D_FF, D_MODEL)),
                  layer((4, D_MODEL))],
        out_specs=tok(D_MODEL),
        out_shape=jax.ShapeDtypeStruct((bsz, n, D_MODEL), F32),
        compiler_params=_params("arbitrary", "arbitrary"),
        name="out_mlp",
    )(x, yssd, ysc, yat, mod, lw["w_out"], lw["w_up"], lw["w_down"], lw["ln"])


def _rope_tables(n):
    rows = n // GRID_W
    row = jnp.repeat(jnp.arange(rows, dtype=F32), GRID_W)
    col = jnp.tile(jnp.arange(GRID_W, dtype=F32), rows)
    inv = ROPE_BASE ** (-jnp.arange(ROPE_F, dtype=F32) / ROPE_F)
    ang = jnp.stack([row[:, None] * inv, col[:, None] * inv], axis=1)
    ang = jnp.repeat(ang[:, :, None, :], 2, axis=2).reshape(n, DIFF_HD)
    sign = jnp.tile(jnp.concatenate([-jnp.ones(ROPE_F, F32), jnp.ones(ROPE_F, F32)]), 2)
    cos_t = jnp.tile(jnp.cos(ang), (1, 2 * DIFF_HEADS))
    sin_t = jnp.tile(jnp.sin(ang) * sign, (1, 2 * DIFF_HEADS))
    return cos_t, sin_t


def _layer_weights(w_in, ssd_conv_w, ssd_conv_b, ssd_dt_bias, ssd_a_log, ssd_d, ssd_norm_w, sc_conv_w,
                   diff_lambda, diff_norm_w, w_out, ln1_g, ln1_b, w_up, w_down, ln2_g, ln2_b):
    w_in_p = _permute_w_in(jnp.swapaxes(w_in, 1, 2))
    prm = jnp.zeros((DEPTH, 8, 128), F32)
    prm = prm.at[:, 0, :16].set(ssd_dt_bias.reshape(DEPTH, 16)).at[:, 1, :16].set(ssd_a_log.reshape(DEPTH, 16))
    return {
        "w_in": w_in_p,
        "conv_w": ssd_conv_w,
        "conv_b": ssd_conv_b.reshape(DEPTH, 1, SSD_CONV_CH),
        "sc_w": sc_conv_w,
        "prm": prm,
        "dexp": jnp.repeat(ssd_d, SSD_HEAD_DIM, axis=1).reshape(DEPTH, 1, SSD_INNER),
        "ssd_nw": ssd_norm_w.reshape(DEPTH, 1, SSD_INNER),
        "emat": _expand_matrix(),
        "lp": diff_lambda,
        "attn_nw": jnp.tile(diff_norm_w, (1, DIFF_HEADS)).reshape(DEPTH, 1, DIFF_WIDTH),
        "ones": _head_ones(),
        "w_out": w_out.astype(BF16),
        "w_up": w_up.astype(BF16),
        "w_down": w_down.astype(BF16),
        "ln": jnp.stack([ln1_g, ln1_b, ln2_g, ln2_b], axis=1),
    }


def _trunk_layer(x, mod, lw, l, lam_init, rope, prefix, init, caches, state_bufs, tm):
    bsz, n, _ = x.shape
    res = _inproj_call(x, mod, lw, l, rope, caches, tm)
    z, xs, bm, cm, dt, ysc, q, k, v = res
    ssd = _ssd_call(xs, bm, cm, dt, z, lw, l, init, state_bufs)
    yat = _attn_call(q, k, v, prefix, lw, l, lam_init, from_cache=caches is not None, tq=min(n, 512))
    if mod.shape[0] == 1:
        flat = lambda t: t.reshape(1, bsz * n, t.shape[-1])
        y = _mlp_call(flat(x), flat(ssd[0]), flat(ysc), flat(yat), mod, lw, l, 512).reshape(bsz, n, D_MODEL)
    else:
        y = _mlp_call(x, ssd[0], ysc, yat, mod, lw, l, 1024)
    new_caches = (k, v) if caches is not None else None
    new_states = (ssd[1], ssd[2]) if state_bufs is not None else None
    return y, new_caches, new_states


def kernel(x_prompt, x_sample, cache_k, cache_v, state_ssm_fwd, state_ssm_bwd, c, c_ctx, w_mod, b_mod, w_in,
           ssd_conv_w, ssd_conv_b, ssd_dt_bias, ssd_a_log, ssd_d, ssd_norm_w, sc_conv_w, diff_lambda,
           diff_norm_w, w_out, ln1_g, ln1_b, w_up, w_down, ln2_g, ln2_b):
    bp, seq, _ = x_prompt.shape
    bs, dec_seq, _ = x_sample.shape
    past = cache_k.shape[2]

    cvec = jnp.zeros((16, D_MODEL), F32).at[:bs].set(c).at[bs].set(c_ctx)
    mods = _mod_call(cvec, w_mod, b_mod)
    rope = _rope_tables(dec_seq)
    lw = _layer_weights(w_in, ssd_conv_w, ssd_conv_b, ssd_dt_bias, ssd_a_log, ssd_d, ssd_norm_w, sc_conv_w,
                        diff_lambda, diff_norm_w, w_out, ln1_g, ln1_b, w_up, w_down, ln2_g, ln2_b)
    prefix = (cache_k.transpose(0, 1, 3, 4, 2).reshape(bs, DEPTH, DIFF_WIDTH, past),
              cache_v.transpose(0, 1, 3, 4, 2).reshape(bs, DEPTH, DIFF_WIDTH, past))
    init = (state_ssm_fwd.reshape(bs, DEPTH, SSD_INNER, SSD_STATE),
            state_ssm_bwd.reshape(bs, DEPTH, SSD_INNER, SSD_STATE))
    caches = (jax.ShapeDtypeStruct((bp, DEPTH, seq, DIFF_WIDTH), F32),) * 2
    states = (jax.ShapeDtypeStruct((bp, DEPTH, SSD_INNER, SSD_STATE), F32),) * 2

    hp, hs = x_prompt, x_sample
    for l in range(DEPTH):
        lam_init = 0.8 - 0.6 * math.exp(-0.3 * l)
        mod_ctx = mods[l, bs:bs + 1].reshape(1, 6, D_MODEL)
        mod_lat = mods[l, :bs].reshape(bs, 6, D_MODEL)
        hp, caches, states = _trunk_layer(hp, mod_ctx, lw, l, lam_init, None, None, None, caches, states, seq)
        hs = _trunk_layer(hs, mod_lat, lw, l, lam_init, rope, prefix, init, None, None, 1024)[0]

    return (hp, hs,
            caches[0].reshape(bp, DEPTH, seq, 2 * DIFF_HEADS, DIFF_HD),
            caches[1].reshape(bp, DEPTH, seq, DIFF_HEADS, DIFF_VD),
            states[0].reshape(bp, DEPTH, SSD_HEADS, SSD_HEAD_DIM, SSD_STATE),
            states[1].reshape(bp, DEPTH, SSD_HEADS, SSD_HEAD_DIM, SSD_STATE))
```

```python
import functools
import math

import jax
import jax.numpy as jnp
import numpy as np
from jax import lax
from jax.experimental import pallas as pl
from jax.experimental.pallas import tpu as pltpu

F32 = jnp.float32
BF16 = jnp.bfloat16

D_MODEL = 1024
DEPTH = 4
GRID_W = 64
SSD_INNER = 512
SSD_HEAD_DIM = 64
SSD_HEADS = 8
SSD_STATE = 128
SSD_CHUNK = 128
SSD_CONV = 5
SSD_CONV_CH = 1024
SC_WIDTH = 256
SC_CONV = 3
DIFF_WIDTH = 256
DIFF_HEADS = 4
DIFF_HD = 32
DIFF_VD = 64
ROPE_BASE = 10000.0
ROPE_F = 8
D_FF = 4096
ALPHA = (2 * DEPTH) ** 0.25
LN_EPS = 1e-5
RMS_EPS = 1e-6

C_Z = 0
C_XBC = 512
C_SCB = 1536
C_SCC = 1792
C_Q = 2304
C_K = 2560
C_V = 2816
C_DT = 3072
IN_COLS = 3088
IN_COLS_PAD = 3200

HALO = 8
VMEM_LIMIT = 56 * 1024 * 1024
NEG_BIG = -1e30
Q_PRESCALE = DIFF_HD ** -0.5 * math.log2(math.e)

_NT = (((1,), (1,)), ((), ()))
_TN = (((0,), (0,)), ((), ()))


def _split2(x):
    hi = x.astype(BF16)
    lo = (x - hi.astype(F32)).astype(BF16)
    return hi, lo


def _split3(x):
    hi = x.astype(BF16)
    r1 = x - hi.astype(F32)
    mid = r1.astype(BF16)
    lo = (r1 - mid.astype(F32)).astype(BF16)
    return hi, mid, lo


def _silu(x):
    return x * jax.nn.sigmoid(x)


def _layer_norm(x, g, b):
    mu = jnp.mean(x, axis=-1, keepdims=True)
    xc = x - mu
    var = jnp.mean(xc * xc, axis=-1, keepdims=True)
    return xc * lax.rsqrt(var + LN_EPS) * g + b


def _params(*sem):
    return pltpu.CompilerParams(dimension_semantics=sem, vmem_limit_bytes=VMEM_LIMIT)


def _mod_kernel(c_ref, w_ref, b_ref, o_ref):
    a = _silu(c_ref[...]).astype(BF16)
    o_ref[0] = jnp.dot(a, w_ref[0].astype(BF16), preferred_element_type=F32) + b_ref[0]


def _mod_call(cvec, w_mod, b_mod):
    tn = 1536
    return pl.pallas_call(
        _mod_kernel,
        grid=(DEPTH, 6 * D_MODEL // tn),
        in_specs=[
            pl.BlockSpec((16, D_MODEL), lambda l, j: (0, 0)),
            pl.BlockSpec((1, D_MODEL, tn), lambda l, j: (l, 0, j)),
            pl.BlockSpec((1, 1, tn), lambda l, j: (l, 0, j)),
        ],
        out_specs=pl.BlockSpec((1, 16, tn), lambda l, j: (l, 0, j)),
        out_shape=jax.ShapeDtypeStruct((DEPTH, 16, 6 * D_MODEL), F32),
        compiler_params=_params("arbitrary", "arbitrary"),
        name="mod_vectors",
    )(cvec, w_mod, b_mod.reshape(DEPTH, 1, 6 * D_MODEL))


def _permute_kernel(w_ref, o_ref, vt_ref, wt_scr):
    cols = w_ref.shape[-1]
    wt_scr[0:C_SCB, :] = w_ref[0, 0:C_SCB, :].astype(BF16)
    wt_scr[C_SCB:C_DT, :] = w_ref[0, C_SCB + 16:IN_COLS, :].astype(BF16)
    wt_scr[C_DT:C_DT + 16, :] = w_ref[0, C_SCB:C_SCB + 16, :].astype(BF16)
    wt_scr[C_DT + 16:IN_COLS_PAD, :] = jnp.zeros((IN_COLS_PAD - IN_COLS, cols), BF16)
    vt_ref[0] = wt_scr[C_V:C_DT, :]
    eye = jnp.where(lax.broadcasted_iota(jnp.int32, (cols, cols), 0)
                    == lax.broadcasted_iota(jnp.int32, (cols, cols), 1), 1.0, 0.0).astype(BF16)
    o_ref[0] = lax.dot_general(eye, wt_scr[...], _NT, preferred_element_type=F32).astype(BF16)


def _permute_w_in(w_in_t):
    tc = 256
    return pl.pallas_call(
        _permute_kernel,
        grid=(DEPTH, D_MODEL // tc),
        in_specs=[pl.BlockSpec((1, IN_COLS, tc), lambda l, i: (l, 0, i))],
        out_specs=(pl.BlockSpec((1, tc, IN_COLS_PAD), lambda l, i: (l, i, 0)),
                   pl.BlockSpec((1, DIFF_WIDTH, tc), lambda l, i: (l, 0, i))),
        out_shape=(jax.ShapeDtypeStruct((DEPTH, D_MODEL, IN_COLS_PAD), BF16),
                   jax.ShapeDtypeStruct((DEPTH, DIFF_WIDTH, D_MODEL), BF16)),
        scratch_shapes=[pltpu.VMEM((IN_COLS_PAD, tc), BF16)],
        compiler_params=_params("arbitrary", "arbitrary"),
        name="permute_w_in",
    )(w_in_t)


def _inproj_kernel(*refs, tm, n_tiles, use_rope, cache_out):
    refs = list(refs)
    x_ref, xp_ref, xn_ref, mod_ref, w_ref, cw_ref, cb_ref, scw_ref = refs[:8]
    pos = 8
    if use_rope:
        cos_ref, sin_ref = refs[pos:pos + 2]
        pos += 2
    if cache_out:
        wvt_ref = None
        pos += 2
    else:
        wvt_ref = refs[pos]
        pos += 1
    z_ref, xs_ref, b_ref, c_ref, dt_ref, ysc_ref, q_ref, k_ref, v_ref = refs[pos:pos + 9]
    i = pl.program_id(1)
    sh1 = mod_ref[0, 0:1, :]
    one_sc1 = 1.0 + mod_ref[0, 1:2, :]
    rows_e = tm + 2 * HALO
    xe = jnp.concatenate([xp_ref[0], x_ref[0], xn_ref[0]], axis=0)
    ue = (xe * one_sc1 + sh1).astype(BF16)
    u = (x_ref[0] * one_sc1 + sh1).astype(BF16)
    rowid = lax.broadcasted_iota(jnp.int32, (rows_e, 1), 0)
    keep = jnp.where(rowid < HALO, jnp.where(i > 0, 1.0, 0.0),
                     jnp.where(rowid >= tm + HALO, jnp.where(i < n_tiles - 1, 1.0, 0.0), 1.0))

    def mm(a, lo, hi):
        return jnp.dot(a, w_ref[0, :, lo:hi], preferred_element_type=F32)

    def shifted(t, off):
        r = t if off == 0 else pltpu.roll(t, (rows_e - off) % rows_e, 0)
        return r[HALO:HALO + tm]

    z_ref[0] = mm(u, C_Z, C_XBC)
    dt_ref[0] = mm(u, C_DT, IN_COLS_PAD)

    xbc = mm(ue, C_XBC, C_SCB) * keep
    acc = cw_ref[0, 0:1, :] * shifted(xbc, -2) + cb_ref[0]
    for kk in range(1, SSD_CONV):
        acc = acc + cw_ref[0, kk:kk + 1, :] * shifted(xbc, kk - 2)
    act = _silu(acc)
    xs_ref[0] = act[:, :SSD_INNER]
    b_ref[0] = act[:, SSD_INNER:SSD_INNER + 256].astype(b_ref.dtype)
    c_ref[0] = act[:, SSD_INNER + 256:].astype(c_ref.dtype)

    scb = mm(u, C_SCB, C_SCC)
    ch = mm(ue, C_SCC, C_Q)
    prod = ch[:, :SC_WIDTH] * ch[:, SC_WIDTH:] * keep
    conv = scw_ref[0, 0:1, :] * shifted(prod, -1)
    for kk in range(1, SC_CONV):
        conv = conv + scw_ref[0, kk:kk + 1, :] * shifted(prod, kk - 1)
    ysc_ref[0] = scb * conv

    q = mm(u, C_Q, C_K)
    k = mm(u, C_K, C_V)
    if use_rope:
        cos = cos_ref[...]
        sin = sin_ref[...]
        lane = lax.broadcasted_iota(jnp.int32, (tm, DIFF_WIDTH), 1)
        first = (lane & 15) < 8

        def rope(t):
            sw = jnp.where(first, pltpu.roll(t, DIFF_WIDTH - 8, 1), pltpu.roll(t, 8, 1))
            return t * cos + sw * sin

        q = rope(q)
        k = rope(k)
    q_ref[0] = (q * Q_PRESCALE).astype(q_ref.dtype)
    if cache_out:
        k_ref[0, 0] = k
        v_ref[0, 0] = mm(u, C_V, C_DT)
    else:
        k_ref[0] = k.astype(k_ref.dtype)
        v_ref[0] = lax.dot_general(wvt_ref[0], u, _NT, preferred_element_type=F32).astype(v_ref.dtype)


def _inproj_call(x, mod, lw, l, rope, caches, tm):
    bsz, n, _ = x.shape
    n_tiles = n // tm
    tpb = tm // 8
    bm = mod.shape[0]
    mod_idx = (lambda b, i: (b, 0, 0)) if bm > 1 else (lambda b, i: (0, 0, 0))
    tok = lambda w: pl.BlockSpec((1, tm, w), lambda b, i: (b, i, 0))
    layer = lambda shp: pl.BlockSpec((1,) + shp, lambda b, i: (l,) + (0,) * len(shp))
    in_specs = [
        tok(D_MODEL),
        pl.BlockSpec((1, HALO, D_MODEL), lambda b, i: (b, jnp.maximum(i * tpb - 1, 0), 0)),
        pl.BlockSpec((1, HALO, D_MODEL), lambda b, i: (b, jnp.minimum((i + 1) * tpb, n // 8 - 1), 0)),
        pl.BlockSpec((1, 6, D_MODEL), mod_idx),
        layer((D_MODEL, IN_COLS_PAD)),
        layer((SSD_CONV, SSD_CONV_CH)),
        layer((1, SSD_CONV_CH)),
        layer((SC_CONV, SC_WIDTH)),
    ]
    args = [x, x, x, mod, lw["w_in"], lw["conv_w"], lw["conv_b"], lw["sc_w"]]
    use_rope = rope is not None
    if use_rope:
        in_specs += [pl.BlockSpec((tm, DIFF_WIDTH), lambda b, i: (i, 0))] * 2
        args += list(rope)
    sds = lambda w, dt: jax.ShapeDtypeStruct((bsz, n, w), dt)
    out_shape = [sds(512, F32), sds(512, F32), sds(256, BF16), sds(256, BF16), sds(128, F32),
                 sds(256, F32), sds(256, BF16)]
    out_specs = [tok(s.shape[-1]) for s in out_shape]
    aliases = {}
    cache_out = caches is not None
    if cache_out:
        aliases = {len(args): 7, len(args) + 1: 8}
        in_specs += [pl.BlockSpec(memory_space=pl.ANY)] * 2
        args += list(caches)
        out_shape += [jax.ShapeDtypeStruct(caches[0].shape, F32)] * 2
        out_specs += [pl.BlockSpec((1, 1, tm, DIFF_WIDTH), lambda b, i: (b, l, i, 0))] * 2
    else:
        in_specs += [layer((DIFF_WIDTH, D_MODEL))]
        args += [lw["w_vt"]]
        out_shape += [sds(256, BF16), jax.ShapeDtypeStruct((bsz, DIFF_WIDTH, n), BF16)]
        out_specs += [tok(256), pl.BlockSpec((1, DIFF_WIDTH, tm), lambda b, i: (b, 0, i))]
    return pl.pallas_call(
        functools.partial(_inproj_kernel, tm=tm, n_tiles=n_tiles, use_rope=use_rope, cache_out=cache_out),
        grid=(bsz, n_tiles),
        in_specs=in_specs,
        out_specs=tuple(out_specs),
        out_shape=tuple(out_shape),
        input_output_aliases=aliases,
        compiler_params=_params("arbitrary", "arbitrary"),
        name="in_proj_rope" if use_rope else "in_proj",
    )(*args)


def _expand_matrix():
    e = np.zeros((256, 3 * 1024), np.float32)
    for part in range(2):
        for qn in range(3):
            for d in range(2):
                for h in range(SSD_HEADS):
                    r = part * 128 + qn * 16 + d * 8 + h
                    c0 = qn * 1024 + d * 512 + h * 64
                    e[r, c0:c0 + 64] = 1.0
    return jnp.asarray(e, dtype=BF16)


def _ssd_kernel(*refs, nc, has_init, emit_state):
    refs = list(refs)
    xs_ref, b_ref, c_ref, dt_ref, z_ref, prm_ref, dexp_ref, nw_ref, emat_ref = refs[:9]
    pos = 9
    if has_init:
        s0f_ref, s0b_ref = refs[pos:pos + 2]
        pos += 2
    if emit_state:
        pos += 2
    y_ref = refs[pos]
    pos += 1
    if emit_state:
        sf_ref, sb_ref = refs[pos:pos + 2]
        pos += 2
    st_scr, gxf_scr, gxb_scr, xdtf_scr, xdtb_scr, xdf_scr, xdb_scr, acum_scr = refs[pos:pos + 8]
    grp = 4 if nc % 4 == 0 else (2 if nc % 2 == 0 else 1)

    lane = lax.broadcasted_iota(jnp.int32, (1, 128), 1)
    ii = lax.broadcasted_iota(jnp.int32, (128, 128), 0)
    jj = lax.broadcasted_iota(jnp.int32, (128, 128), 1)
    low = jj <= ii
    upp = jj >= ii
    ltri = jnp.where(low, 1.0, 0.0).astype(BF16)
    bias = prm_ref[0, 0:1, :]
    a_row = -jnp.exp(prm_ref[0, 1:2, :])
    is_head = lane < 2 * SSD_HEADS
    is_fwd = lane < SSD_HEADS
    half = lane < 64

    for blk in range(4):
        bl = slice(blk * 128, (blk + 1) * 128)
        if has_init:
            st_scr[0, :, bl] = s0f_ref[0, 0, bl, :].T
            st_scr[1, :, bl] = s0b_ref[0, 0, bl, :].T
        else:
            st_scr[0, :, bl] = jnp.zeros((SSD_STATE, 128), F32)
            st_scr[1, :, bl] = jnp.zeros((SSD_STATE, 128), F32)

    def prep_body(t, carry):
        r0 = pl.multiple_of(t * (grp * SSD_CHUNK), grp * SSD_CHUNK)
        rows = pl.ds(r0, grp * SSD_CHUNK)
        dtv = jnp.where(is_head, jax.nn.softplus(dt_ref[0, rows, :] + bias), 0.0)
        dta = dtv * a_row
        chunk = lambda v, c: v[c * SSD_CHUNK:(c + 1) * SSD_CHUNK]
        parts = []
        for c in range(grp):
            parts += list(_split3(chunk(dta, c)))
        a3 = jnp.dot(ltri, jnp.concatenate(parts, axis=1), preferred_element_type=F32)
        acums, packs = [], []
        for c in range(grp):
            o = c * 384
            cum = a3[:, o:o + 128] + a3[:, o + 128:o + 256] + a3[:, o + 256:o + 384]
            total = cum[127:128, :]
            acum = jnp.where(is_fwd, cum, total - cum + chunk(dta, c))
            gdec = jnp.where(is_head, jnp.exp(acum), 0.0)
            wdec = chunk(dtv, c) * jnp.exp(total - acum)
            acums.append(acum)
            packs.append(chunk(dtv, c) + pltpu.roll(wdec, 16, 1) + pltpu.roll(gdec, 32, 1))
        packed = jnp.concatenate(packs, axis=0)
        x6 = jnp.dot(jnp.concatenate(_split2(packed), axis=1), emat_ref[...], preferred_element_type=F32)
        x = xs_ref[0, rows, :]
        xdtf_scr[rows, :] = (x * x6[:, 0:512]).astype(BF16)
        xdtb_scr[rows, :] = (x * x6[:, 512:1024]).astype(BF16)
        xdf_scr[rows, :] = (x * x6[:, 1024:1536]).astype(BF16)
        xdb_scr[rows, :] = (x * x6[:, 1536:2048]).astype(BF16)
        gxf_scr[rows, :] = x6[:, 2048:2560]
        gxb_scr[rows, :] = x6[:, 2560:3072]
        acum_scr[rows, :] = jnp.concatenate(acums, axis=0)
        return carry

    lax.fori_loop(0, nc // grp, prep_body, 0)

    def fwd_body(c, carry):
        r0 = pl.multiple_of(c * SSD_CHUNK, SSD_CHUNK)
        rows = pl.ds(r0, SSD_CHUNK)
        acum = acum_scr[rows, :]
        acum_t = acum.T
        gx_f = gxf_scr[rows, :]
        xdt_f = xdtf_scr[rows, :]
        xdt_b = xdtb_scr[rows, :]
        xd_f = xdf_scr[rows, :]
        bc = b_ref[0, rows, :]
        cc = c_ref[0, rows, :]

        y_parts = []
        for g in range(2):
            gl = slice(g * 256, (g + 1) * 256)
            bg = bc[:, g * 128:(g + 1) * 128]
            cg = cc[:, g * 128:(g + 1) * 128]
            cb = lax.dot_general(cg, bg, _NT, preferred_element_type=F32)
            s_f = st_scr[0, :, gl]
            yo_f = jnp.dot(cg, s_f.astype(BF16), preferred_element_type=F32) * gx_f[:, gl]
            st_scr[0, :, gl] = s_f * gx_f[127:128, gl] + lax.dot_general(
                bg, xd_f[:, gl], _TN, preferred_element_type=F32)
            for pair in range(2):
                hp = 2 * g + pair
                pl_ = slice(hp * 128, (hp + 1) * 128)
                rhs = jnp.concatenate([xdt_f[:, pl_], xdt_b[:, pl_]], axis=0)
                ys = []
                for sub in range(2):
                    h = 2 * hp + sub
                    hb = SSD_HEADS + h
                    seg_f = acum[:, h:h + 1] - acum_t[h:h + 1, :]
                    m_f = jnp.exp(jnp.where(low, seg_f, NEG_BIG)) * cb
                    seg_b = acum[:, hb:hb + 1] - acum_t[hb:hb + 1, :]
                    m_b = jnp.exp(jnp.where(upp, seg_b, NEG_BIG)) * cb
                    lhs = jnp.concatenate([m_f, m_b], axis=1).astype(BF16)
                    ys.append(jnp.dot(lhs, rhs, preferred_element_type=F32))
                y_pair = jnp.where(half, ys[0], ys[1])
                y_parts.append(y_pair + yo_f[:, pair * 128:(pair + 1) * 128])
        y_ref[0, rows, :] = jnp.concatenate(y_parts, axis=1)
        return carry

    lax.fori_loop(0, nc, fwd_body, 0, unroll=4 if nc % 4 == 0 else 2)
    if emit_state:
        for blk in range(4):
            bl = slice(blk * 128, (blk + 1) * 128)
            sf_ref[0, 0, bl, :] = st_scr[0, :, bl].T

    dexp = dexp_ref[0]
    nw = nw_ref[0]

    def bwd_body(t, carry):
        c = nc - 1 - t
        r0 = pl.multiple_of(c * SSD_CHUNK, SSD_CHUNK)
        rows = pl.ds(r0, SSD_CHUNK)
        bc = b_ref[0, rows, :]
        cc = c_ref[0, rows, :]
        gx_b = gxb_scr[rows, :]
        xd_b = xdb_scr[rows, :]
        yo = []
        for g in range(2):
            gl = slice(g * 256, (g + 1) * 256)
            bg = bc[:, g * 128:(g + 1) * 128]
            cg = cc[:, g * 128:(g + 1) * 128]
            s_b = st_scr[1, :, gl]
            yo.append(jnp.dot(cg, s_b.astype(BF16), preferred_element_type=F32) * gx_b[:, gl])
            st_scr[1, :, gl] = s_b * gx_b[0:1, gl] + lax.dot_general(
                bg, xd_b[:, gl], _TN, preferred_element_type=F32)
        x = xs_ref[0, rows, :]
        y = y_ref[0, rows, :] + jnp.concatenate(yo, axis=1) + dexp * x
        y = y * _silu(z_ref[0, rows, :])
        ms = jnp.mean(y * y, axis=-1, keepdims=True)
        y_ref[0, rows, :] = y * lax.rsqrt(ms + RMS_EPS) * nw
        return carry

    lax.fori_loop(0, nc, bwd_body, 0, unroll=4 if nc % 4 == 0 else 2)
    if emit_state:
        for blk in range(4):
            bl = slice(blk * 128, (blk + 1) * 128)
            sb_ref[0, 0, bl, :] = st_scr[1, :, bl].T


def _ssd_call(xs, bm, cm, dt, z, lw, l, init, state_bufs):
    bsz, n, _ = xs.shape
    nc = n // SSD_CHUNK
    seq = lambda w: pl.BlockSpec((1, n, w), lambda b: (b, 0, 0))
    st = pl.BlockSpec((1, 1, SSD_INNER, SSD_STATE), lambda b: (b, l, 0, 0))
    layer = lambda shp: pl.BlockSpec((1,) + shp, lambda b: (l,) + (0,) * len(shp))
    in_specs = [seq(512), seq(256), seq(256), seq(128), seq(512),
                layer((8, 128)), layer((1, 512)), layer((1, 512)),
                pl.BlockSpec((256, 3072), lambda b: (0, 0))]
    args = [xs, bm, cm, dt, z, lw["prm"], lw["dexp"], lw["ssd_nw"], lw["emat"]]
    has_init = init is not None
    emit_state = state_bufs is not None
    if has_init:
        in_specs += [st, st]
        args += list(init)
    out_shape = [jax.ShapeDtypeStruct((bsz, n, SSD_INNER), F32)]
    out_specs = [seq(512)]
    aliases = {}
    if emit_state:
        aliases = {len(args): 1, len(args) + 1: 2}
        in_specs += [pl.BlockSpec(memory_space=pl.ANY)] * 2
        args += list(state_bufs)
        out_shape += [jax.ShapeDtypeStruct(state_bufs[0].shape, F32)] * 2
        out_specs += [st, st]
    return pl.pallas_call(
        functools.partial(_ssd_kernel, nc=nc, has_init=has_init, emit_state=emit_state),
        grid=(bsz,),
        in_specs=in_specs,
        out_specs=tuple(out_specs),
        out_shape=tuple(out_shape),
        input_output_aliases=aliases,
        scratch_shapes=[pltpu.VMEM((2, SSD_STATE, SSD_INNER), F32)]
                       + [pltpu.VMEM((n, SSD_INNER), F32)] * 2
                       + [pltpu.VMEM((n, SSD_INNER), BF16)] * 4
                       + [pltpu.VMEM((n, 128), F32)],
        compiler_params=_params("arbitrary"),
        name="ssd_scan_init" if has_init else "ssd_scan",
    )(*args)


def _attn_kernel(*refs, lam_init, has_prefix, from_cache):
    refs = list(refs)
    q_ref, k_ref, v_ref = refs[:3]
    pos = 3
    if has_prefix:
        kp_ref, vp_ref = refs[pos:pos + 2]
        pos += 2
    lp_ref, nw_ref, ones_ref, o_ref = refs[pos:pos + 4]

    q = q_ref[0]
    tq = q.shape[0]
    if from_cache:
        segs = [(k_ref[0, 0].astype(BF16), v_ref[0, 0].T.astype(BF16))]
    else:
        segs = [(k_ref[0], v_ref[0])]
    if has_prefix:
        segs.append((kp_ref[0, 0].T.astype(BF16), vp_ref[0, 0].astype(BF16)))
    lp = lp_ref[0]
    lam = (jnp.exp(jnp.sum(lp[0:1] * lp[1:2], axis=-1, keepdims=True))
           - jnp.exp(jnp.sum(lp[2:3] * lp[3:4], axis=-1, keepdims=True)) + lam_init)
    lane = lax.broadcasted_iota(jnp.int32, (1, 128), 1)
    row = lax.broadcasted_iota(jnp.int32, (128, 1), 0)

    def scores(head):
        g, hh = divmod(head, 2)
        gs = slice(g * 128, (g + 1) * 128)
        qg = q[:, gs]
        qms = []
        for s in range(2):
            lo = (2 * hh + s) * DIFF_HD
            qms.append(jnp.where((lane >= lo) & (lane < lo + DIFF_HD), qg, jnp.zeros_like(qg)))
        qm = jnp.concatenate(qms, axis=0)
        return [lax.dot_general(kk[:, gs], qm, _NT, preferred_element_type=F32) for kk, _ in segs]

    def weighted_values(head, scs):
        g, hh = divmod(head, 2)
        own = slice(DIFF_VD * hh, DIFF_VD * (hh + 1))
        other = DIFF_VD * (1 - hh)
        vmod = []
        for _, vt in segs:
            mine = vt[g * 128 + DIFF_VD * hh:g * 128 + DIFF_VD * (hh + 1), :]
            ones = jnp.ones_like(mine)
            vmod.append(jnp.concatenate([mine, ones] if hh == 0 else [ones, mine], axis=0))
        mx = jnp.max(scs[0], axis=0, keepdims=True)
        for sc in scs[1:]:
            mx = jnp.maximum(mx, jnp.max(sc, axis=0, keepdims=True))
        res = None
        for sc, vm in zip(scs, vmod):
            part = jnp.dot(vm, jnp.exp2(sc - mx).astype(BF16), preferred_element_type=F32)
            res = part if res is None else res + part
        r0, r1 = res[:, :tq], res[:, tq:]
        return r0[own] * (1.0 / r0[other:other + 1]) - r1[own] * (lam / r1[other:other + 1])

    o_heads = []
    pending = scores(0)
    for head in range(DIFF_HEADS):
        nxt = scores(head + 1) if head + 1 < DIFF_HEADS else None
        o_heads.append(weighted_values(head, pending))
        pending = nxt
    o = jnp.concatenate(o_heads, axis=0).T
    ms = jnp.dot(jnp.concatenate(_split2(o * o), axis=1), ones_ref[...],
                 preferred_element_type=F32) * (1.0 / DIFF_VD)
    o_ref[0] = o * lax.rsqrt(ms + RMS_EPS) * nw_ref[0] * (1.0 - lam_init)


def _head_ones():
    e = np.zeros((512, 256), np.float32)
    for part in range(2):
        for h in range(DIFF_HEADS):
            e[part * 256 + h * 64:part * 256 + (h + 1) * 64, h * 64:(h + 1) * 64] = 1.0
    return jnp.asarray(e, dtype=BF16)


def _attn_call(q, k, v, prefix, lw, l, lam_init, from_cache, tq=256):
    bsz, n, _ = q.shape
    layer = lambda shp: pl.BlockSpec((1,) + shp, lambda b, i: (l,) + (0,) * len(shp))
    if from_cache:
        kv_spec = pl.BlockSpec((1, 1, n, DIFF_WIDTH), lambda b, i: (b, l, 0, 0))
    else:
        kv_spec = pl.BlockSpec((1, n, DIFF_WIDTH), lambda b, i: (b, 0, 0))
    v_spec = kv_spec if from_cache else pl.BlockSpec((1, DIFF_WIDTH, n), lambda b, i: (b, 0, 0))
    in_specs = [pl.BlockSpec((1, tq, DIFF_WIDTH), lambda b, i: (b, i, 0)), kv_spec, v_spec]
    args = [q, k, v]
    has_prefix = prefix is not None
    if has_prefix:
        past = prefix[0].shape[3]
        in_specs += [pl.BlockSpec((1, 1, DIFF_WIDTH, past), lambda b, i: (b, l, 0, 0))] * 2
        args += list(prefix)
    in_specs += [layer((4, DIFF_HD)), layer((1, DIFF_WIDTH)), pl.BlockSpec((512, 256), lambda b, i: (0, 0))]
    args += [lw["lp"], lw["attn_nw"], lw["ones"]]
    return pl.pallas_call(
        functools.partial(_attn_kernel, lam_init=lam_init, has_prefix=has_prefix, from_cache=from_cache),
        grid=(bsz, n // tq),
        in_specs=in_specs,
        out_specs=pl.BlockSpec((1, tq, DIFF_WIDTH), lambda b, i: (b, i, 0)),
        out_shape=jax.ShapeDtypeStruct((bsz, n, DIFF_WIDTH), F32),
        compiler_params=_params("arbitrary", "arbitrary"),
        name="diff_attn_prefix" if has_prefix else "diff_attn",
    )(*args)


def _mlp_kernel(x_ref, yssd_ref, ysc_ref, yat_ref, mod_ref, wo_ref, wu_ref, wd_ref, ln_ref, o_ref, *, parts):
    g1 = mod_ref[0, 2:3, :]
    sh2 = mod_ref[0, 3:4, :]
    sc2 = mod_ref[0, 4:5, :]
    g2 = mod_ref[0, 5:6, :]
    rows_per = x_ref.shape[1] // parts

    def mix_norm(r):
        rows = slice(r * rows_per, (r + 1) * rows_per)
        m = (jnp.dot(yssd_ref[0, rows, :].astype(BF16), wo_ref[0, 0:512, :], preferred_element_type=F32)
             + jnp.dot(ysc_ref[0, rows, :].astype(BF16), wo_ref[0, 512:768, :], preferred_element_type=F32)
             + jnp.dot(yat_ref[0, rows, :].astype(BF16), wo_ref[0, 768:1024, :], preferred_element_type=F32))
        x1 = _layer_norm(ALPHA * x_ref[0, rows, :] + g1 * m, ln_ref[0, 0:1, :], ln_ref[0, 1:2, :])
        return x1, (x1 * (1.0 + sc2) + sh2).astype(BF16)

    def mlp_norm(r, x1, h):
        rows = slice(r * rows_per, (r + 1) * rows_per)
        up = jnp.dot(h, wu_ref[0], preferred_element_type=F32)
        act = jnp.square(jnp.maximum(up, 0.0)).astype(BF16)
        f = jnp.dot(act, wd_ref[0], preferred_element_type=F32)
        o_ref[0, rows, :] = _layer_norm(ALPHA * x1 + g2 * f, ln_ref[0, 2:3, :], ln_ref[0, 3:4, :])

    staged = mix_norm(0)
    for r in range(parts):
        nxt = mix_norm(r + 1) if r + 1 < parts else None
        mlp_norm(r, *staged)
        staged = nxt


def _mlp_call(x, yssd, ysc, yat, mod, lw, l, tm):
    bsz, n, _ = x.shape
    bm = mod.shape[0]
    mod_idx = (lambda b, i: (b, 0, 0)) if bm > 1 else (lambda b, i: (0, 0, 0))
    tok = lambda w: pl.BlockSpec((1, tm, w), lambda b, i: (b, i, 0))
    layer = lambda shp: pl.BlockSpec((1,) + shp, lambda b, i: (l,) + (0,) * len(shp),
                                     pipeline_mode=pl.Buffered(1))
    return pl.pallas_call(
        functools.partial(_mlp_kernel, parts=tm // 256),
        grid=(bsz, n // tm),
        in_specs=[tok(D_MODEL), tok(512), tok(256), tok(256),
                  pl.BlockSpec((1, 6, D_MODEL), mod_idx),
                  layer((D_MODEL, D_MODEL)), layer((D_MODEL, D_FF)), layer((D_FF, D_MODEL)),
                  layer((4, D_MODEL))],
        out_specs=tok(D_MODEL),
        out_shape=jax.ShapeDtypeStruct((bsz, n, D_MODEL), F32),
        compiler_params=_params("arbitrary", "arbitrary"),
        name="out_mlp",
    )(x, yssd, ysc, yat, mod, lw["w_out"], lw["w_up"], lw["w_down"], lw["ln"])


def _rope_tables(n):
    rows = n // GRID_W
    row = jnp.repeat(jnp.arange(rows, dtype=F32), GRID_W)
    col = jnp.tile(jnp.arange(GRID_W, dtype=F32), rows)
    inv = ROPE_BASE ** (-jnp.arange(ROPE_F, dtype=F32) / ROPE_F)
    ang = jnp.stack([row[:, None] * inv, col[:, None] * inv], axis=1)
    ang = jnp.repeat(ang[:, :, None, :], 2, axis=2).reshape(n, DIFF_HD)
    sign = jnp.tile(jnp.concatenate([-jnp.ones(ROPE_F, F32), jnp.ones(ROPE_F, F32)]), 2)
    cos_t = jnp.tile(jnp.cos(ang), (1, 2 * DIFF_HEADS))
    sin_t = jnp.tile(jnp.sin(ang) * sign, (1, 2 * DIFF_HEADS))
    return cos_t, sin_t


def _layer_weights(w_in, ssd_conv_w, ssd_conv_b, ssd_dt_bias, ssd_a_log, ssd_d, ssd_norm_w, sc_conv_w,
                   diff_lambda, diff_norm_w, w_out, ln1_g, ln1_b, w_up, w_down, ln2_g, ln2_b):
    w_in_p, w_vt = _permute_w_in(jnp.swapaxes(w_in, 1, 2))
    prm = jnp.zeros((DEPTH, 8, 128), F32)
    prm = prm.at[:, 0, :16].set(ssd_dt_bias.reshape(DEPTH, 16)).at[:, 1, :16].set(ssd_a_log.reshape(DEPTH, 16))
    return {
        "w_in": w_in_p,
        "w_vt": w_vt,
        "conv_w": ssd_conv_w,
        "conv_b": ssd_conv_b.reshape(DEPTH, 1, SSD_CONV_CH),
        "sc_w": sc_conv_w,
        "prm": prm,
        "dexp": jnp.repeat(ssd_d, SSD_HEAD_DIM, axis=1).reshape(DEPTH, 1, SSD_INNER),
        "ssd_nw": ssd_norm_w.reshape(DEPTH, 1, SSD_INNER),
        "emat": _expand_matrix(),
        "lp": diff_lambda,
        "attn_nw": jnp.tile(diff_norm_w, (1, DIFF_HEADS)).reshape(DEPTH, 1, DIFF_WIDTH),
        "ones": _head_ones(),
        "w_out": w_out.astype(BF16),
        "w_up": w_up.astype(BF16),
        "w_down": w_down.astype(BF16),
        "ln": jnp.stack([ln1_g, ln1_b, ln2_g, ln2_b], axis=1),
    }


def _trunk_layer(x, mod, lw, l, lam_init, rope, prefix, init, caches, state_bufs, tm):
    bsz, n, _ = x.shape
    res = _inproj_call(x, mod, lw, l, rope, caches, tm)
    z, xs, bm, cm, dt, ysc, q, k, v = res
    ssd = _ssd_call(xs, bm, cm, dt, z, lw, l, init, state_bufs)
    yat = _attn_call(q, k, v, prefix, lw, l, lam_init, from_cache=caches is not None, tq=min(n, 512))
    if mod.shape[0] == 1:
        flat = lambda t: t.reshape(1, bsz * n, t.shape[-1])
        y = _mlp_call(flat(x), flat(ssd[0]), flat(ysc), flat(yat), mod, lw, l, 512).reshape(bsz, n, D_MODEL)
    else:
        y = _mlp_call(x, ssd[0], ysc, yat, mod, lw, l, 1024)
    new_caches = (k, v) if caches is not None else None
    new_states = (ssd[1], ssd[2]) if state_bufs is not None else None
    return y, new_caches, new_states


def kernel(x_prompt, x_sample, cache_k, cache_v, state_ssm_fwd, state_ssm_bwd, c, c_ctx, w_mod, b_mod, w_in,
           ssd_conv_w, ssd_conv_b, ssd_dt_bias, ssd_a_log, ssd_d, ssd_norm_w, sc_conv_w, diff_lambda,
           diff_norm_w, w_out, ln1_g, ln1_b, w_up, w_down, ln2_g, ln2_b):
    bp, seq, _ = x_prompt.shape
    bs, dec_seq, _ = x_sample.shape
    past = cache_k.shape[2]

    cvec = jnp.zeros((16, D_MODEL), F32).at[:bs].set(c).at[bs].set(c_ctx)
    mods = _mod_call(cvec, w_mod, b_mod)
    rope = _rope_tables(dec_seq)
    lw = _layer_weights(w_in, ssd_conv_w, ssd_conv_b, ssd_dt_bias, ssd_a_log, ssd_d, ssd_norm_w, sc_conv_w,
                        diff_lambda, diff_norm_w, w_out, ln1_g, ln1_b, w_up, w_down, ln2_g, ln2_b)
    prefix = (cache_k.transpose(0, 1, 3, 4, 2).reshape(bs, DEPTH, DIFF_WIDTH, past),
              cache_v.transpose(0, 1, 3, 4, 2).reshape(bs, DEPTH, DIFF_WIDTH, past))
    init = (state_ssm_fwd.reshape(bs, DEPTH, SSD_INNER, SSD_STATE),
            state_ssm_bwd.reshape(bs, DEPTH, SSD_INNER, SSD_STATE))
    caches = (jnp.zeros((bp, DEPTH, seq, DIFF_WIDTH), F32), jnp.zeros((bp, DEPTH, seq, DIFF_WIDTH), F32))
    states = (jnp.zeros((bp, DEPTH, SSD_INNER, SSD_STATE), F32), jnp.zeros((bp, DEPTH, SSD_INNER, SSD_STATE), F32))

    hp, hs = x_prompt, x_sample
    for l in range(DEPTH):
        lam_init = 0.8 - 0.6 * math.exp(-0.3 * l)
        mod_ctx = mods[l, bs:bs + 1].reshape(1, 6, D_MODEL)
        mod_lat = mods[l, :bs].reshape(bs, 6, D_MODEL)
        hp, caches, states = _trunk_layer(hp, mod_ctx, lw, l, lam_init, None, None, None, caches, states, seq)
        hs = _trunk_layer(hs, mod_lat, lw, l, lam_init, rope, prefix, init, None, None, 1024)[0]

    return (hp, hs,
            caches[0].reshape(bp, DEPTH, seq, 2 * DIFF_HEADS, DIFF_HD),
            caches[1].reshape(bp, DEPTH, seq, DIFF_HEADS, DIFF_VD),
            states[0].reshape(bp, DEPTH, SSD_HEADS, SSD_HEAD_DIM, SSD_STATE),
            states[1].reshape(bp, DEPTH, SSD_HEADS, SSD_HEAD_DIM, SSD_STATE))
```

```python
import functools
import math

import jax
import jax.numpy as jnp
import numpy as np
from jax import lax
from jax.experimental import pallas as pl
from jax.experimental.pallas import tpu as pltpu

F32 = jnp.float32
BF16 = jnp.bfloat16

D_MODEL = 1024
DEPTH = 4
GRID_W = 64
SSD_INNER = 512
SSD_HEAD_DIM = 64
SSD_HEADS = 8
SSD_STATE = 128
SSD_CHUNK = 128
SSD_CONV = 5
SSD_CONV_CH = 1024
SC_WIDTH = 256
SC_CONV = 3
DIFF_WIDTH = 256
DIFF_HEADS = 4
DIFF_HD = 32
DIFF_VD = 64
ROPE_BASE = 10000.0
ROPE_F = 8
D_FF = 4096
ALPHA = (2 * DEPTH) ** 0.25
LN_EPS = 1e-5
RMS_EPS = 1e-6

C_Z = 0
C_XBC = 512
C_SCB = 1536
C_SCC = 1792
C_Q = 2304
C_K = 2560
C_V = 2816
C_DT = 3072
IN_COLS = 3088
IN_COLS_PAD = 3200

HALO = 8
VMEM_LIMIT = 56 * 1024 * 1024
NEG_BIG = -1e30
Q_PRESCALE = DIFF_HD ** -0.5 * math.log2(math.e)

_NT = (((1,), (1,)), ((), ()))
_TN = (((0,), (0,)), ((), ()))


def _split2(x):
    hi = x.astype(BF16)
    lo = (x - hi.astype(F32)).astype(BF16)
    return hi, lo


def _split3(x):
    hi = x.astype(BF16)
    r1 = x - hi.astype(F32)
    mid = r1.astype(BF16)
    lo = (r1 - mid.astype(F32)).astype(BF16)
    return hi, mid, lo


def _silu(x):
    return x * jax.nn.sigmoid(x)


def _layer_norm(x, g, b):
    mu = jnp.mean(x, axis=-1, keepdims=True)
    xc = x - mu
    var = jnp.mean(xc * xc, axis=-1, keepdims=True)
    return xc * lax.rsqrt(var + LN_EPS) * g + b


def _params(*sem):
    return pltpu.CompilerParams(dimension_semantics=sem, vmem_limit_bytes=VMEM_LIMIT)


def _mod_kernel(c_ref, w_ref, b_ref, o_ref):
    a = _silu(c_ref[...]).astype(BF16)
    o_ref[0] = jnp.dot(a, w_ref[0].astype(BF16), preferred_element_type=F32) + b_ref[0]


def _mod_call(cvec, w_mod, b_mod):
    tn = 1536
    return pl.pallas_call(
        _mod_kernel,
        grid=(DEPTH, 6 * D_MODEL // tn),
        in_specs=[
            pl.BlockSpec((16, D_MODEL), lambda l, j: (0, 0)),
            pl.BlockSpec((1, D_MODEL, tn), lambda l, j: (l, 0, j)),
            pl.BlockSpec((1, 1, tn), lambda l, j: (l, 0, j)),
        ],
        out_specs=pl.BlockSpec((1, 16, tn), lambda l, j: (l, 0, j)),
        out_shape=jax.ShapeDtypeStruct((DEPTH, 16, 6 * D_MODEL), F32),
        compiler_params=_params("arbitrary", "arbitrary"),
        name="mod_vectors",
    )(cvec, w_mod, b_mod.reshape(DEPTH, 1, 6 * D_MODEL))


def _permute_kernel(w_ref, o_ref, vt_ref, wt_scr):
    cols = w_ref.shape[-1]
    wt_scr[0:C_SCB, :] = w_ref[0, 0:C_SCB, :].astype(BF16)
    wt_scr[C_SCB:C_DT, :] = w_ref[0, C_SCB + 16:IN_COLS, :].astype(BF16)
    wt_scr[C_DT:C_DT + 16, :] = w_ref[0, C_SCB:C_SCB + 16, :].astype(BF16)
    wt_scr[C_DT + 16:IN_COLS_PAD, :] = jnp.zeros((IN_COLS_PAD - IN_COLS, cols), BF16)
    vt_ref[0] = wt_scr[C_V:C_DT, :]
    eye = jnp.where(lax.broadcasted_iota(jnp.int32, (cols, cols), 0)
                    == lax.broadcasted_iota(jnp.int32, (cols, cols), 1), 1.0, 0.0).astype(BF16)
    o_ref[0] = lax.dot_general(eye, wt_scr[...], _NT, preferred_element_type=F32).astype(BF16)


def _permute_w_in(w_in_t):
    tc = 256
    return pl.pallas_call(
        _permute_kernel,
        grid=(DEPTH, D_MODEL // tc),
        in_specs=[pl.BlockSpec((1, IN_COLS, tc), lambda l, i: (l, 0, i))],
        out_specs=(pl.BlockSpec((1, tc, IN_COLS_PAD), lambda l, i: (l, i, 0)),
                   pl.BlockSpec((1, DIFF_WIDTH, tc), lambda l, i: (l, 0, i))),
        out_shape=(jax.ShapeDtypeStruct((DEPTH, D_MODEL, IN_COLS_PAD), BF16),
                   jax.ShapeDtypeStruct((DEPTH, DIFF_WIDTH, D_MODEL), BF16)),
        scratch_shapes=[pltpu.VMEM((IN_COLS_PAD, tc), BF16)],
        compiler_params=_params("arbitrary", "arbitrary"),
        name="permute_w_in",
    )(w_in_t)


def _inproj_kernel(*refs, tm, n_tiles, use_rope, cache_out):
    refs = list(refs)
    x_ref, xp_ref, xn_ref, mod_ref, w_ref, cw_ref, cb_ref, scw_ref = refs[:8]
    pos = 8
    if use_rope:
        cos_ref, sin_ref = refs[pos:pos + 2]
        pos += 2
    if cache_out:
        wvt_ref = None
        pos += 2
    else:
        wvt_ref = refs[pos]
        pos += 1
    z_ref, xs_ref, b_ref, c_ref, dt_ref, ysc_ref, q_ref, k_ref, v_ref = refs[pos:pos + 9]
    i = pl.program_id(1)
    sh1 = mod_ref[0, 0:1, :]
    one_sc1 = 1.0 + mod_ref[0, 1:2, :]
    rows_e = tm + 2 * HALO
    xe = jnp.concatenate([xp_ref[0], x_ref[0], xn_ref[0]], axis=0)
    ue = (xe * one_sc1 + sh1).astype(BF16)
    u = (x_ref[0] * one_sc1 + sh1).astype(BF16)
    rowid = lax.broadcasted_iota(jnp.int32, (rows_e, 1), 0)
    keep = jnp.where(rowid < HALO, jnp.where(i > 0, 1.0, 0.0),
                     jnp.where(rowid >= tm + HALO, jnp.where(i < n_tiles - 1, 1.0, 0.0), 1.0))

    def mm(a, lo, hi):
        return jnp.dot(a, w_ref[0, :, lo:hi], preferred_element_type=F32)

    def shifted(t, off):
        r = t if off == 0 else pltpu.roll(t, (rows_e - off) % rows_e, 0)
        return r[HALO:HALO + tm]

    z_ref[0] = mm(u, C_Z, C_XBC)
    dt_ref[0] = mm(u, C_DT, IN_COLS_PAD)

    xbc = mm(ue, C_XBC, C_SCB) * keep
    acc = cw_ref[0, 0:1, :] * shifted(xbc, -2) + cb_ref[0]
    for kk in range(1, SSD_CONV):
        acc = acc + cw_ref[0, kk:kk + 1, :] * shifted(xbc, kk - 2)
    act = _silu(acc)
    xs_ref[0] = act[:, :SSD_INNER]
    b_ref[0] = act[:, SSD_INNER:SSD_INNER + 256].astype(b_ref.dtype)
    c_ref[0] = act[:, SSD_INNER + 256:].astype(c_ref.dtype)

    scb = mm(u, C_SCB, C_SCC)
    ch = mm(ue, C_SCC, C_Q)
    prod = ch[:, :SC_WIDTH] * ch[:, SC_WIDTH:] * keep
    conv = scw_ref[0, 0:1, :] * shifted(prod, -1)
    for kk in range(1, SC_CONV):
        conv = conv + scw_ref[0, kk:kk + 1, :] * shifted(prod, kk - 1)
    ysc_ref[0] = scb * conv

    q = mm(u, C_Q, C_K)
    k = mm(u, C_K, C_V)
    if use_rope:
        cos = cos_ref[...]
        sin = sin_ref[...]
        lane = lax.broadcasted_iota(jnp.int32, (tm, DIFF_WIDTH), 1)
        first = (lane & 15) < 8

        def rope(t):
            sw = jnp.where(first, pltpu.roll(t, DIFF_WIDTH - 8, 1), pltpu.roll(t, 8, 1))
            return t * cos + sw * sin

        q = rope(q)
        k = rope(k)
    q_ref[0] = (q * Q_PRESCALE).astype(q_ref.dtype)
    if cache_out:
        k_ref[0, 0] = k
        v_ref[0, 0] = mm(u, C_V, C_DT)
    else:
        k_ref[0] = k.astype(k_ref.dtype)
        v_ref[0] = lax.dot_general(wvt_ref[0], u, _NT, preferred_element_type=F32).astype(v_ref.dtype)


def _inproj_call(x, mod, lw, l, rope, caches, tm):
    bsz, n, _ = x.shape
    n_tiles = n // tm
    tpb = tm // 8
    bm = mod.shape[0]
    mod_idx = (lambda b, i: (b, 0, 0)) if bm > 1 else (lambda b, i: (0, 0, 0))
    tok = lambda w: pl.BlockSpec((1, tm, w), lambda b, i: (b, i, 0))
    layer = lambda shp: pl.BlockSpec((1,) + shp, lambda b, i: (l,) + (0,) * len(shp))
    in_specs = [
        tok(D_MODEL),
        pl.BlockSpec((1, HALO, D_MODEL), lambda b, i: (b, jnp.maximum(i * tpb - 1, 0), 0)),
        pl.BlockSpec((1, HALO, D_MODEL), lambda b, i: (b, jnp.minimum((i + 1) * tpb, n // 8 - 1), 0)),
        pl.BlockSpec((1, 6, D_MODEL), mod_idx),
        layer((D_MODEL, IN_COLS_PAD)),
        layer((SSD_CONV, SSD_CONV_CH)),
        layer((1, SSD_CONV_CH)),
        layer((SC_CONV, SC_WIDTH)),
    ]
    args = [x, x, x, mod, lw["w_in"], lw["conv_w"], lw["conv_b"], lw["sc_w"]]
    use_rope = rope is not None
    if use_rope:
        in_specs += [pl.BlockSpec((tm, DIFF_WIDTH), lambda b, i: (i, 0))] * 2
        args += list(rope)
    sds = lambda w, dt: jax.ShapeDtypeStruct((bsz, n, w), dt)
    out_shape = [sds(512, F32), sds(512, F32), sds(256, BF16), sds(256, BF16), sds(128, F32),
                 sds(256, F32), sds(256, BF16)]
    out_specs = [tok(s.shape[-1]) for s in out_shape]
    aliases = {}
    cache_out = caches is not None
    if cache_out:
        aliases = {len(args): 7, len(args) + 1: 8}
        in_specs += [pl.BlockSpec(memory_space=pl.ANY)] * 2
        args += list(caches)
        out_shape += [jax.ShapeDtypeStruct(caches[0].shape, F32)] * 2
        out_specs += [pl.BlockSpec((1, 1, tm, DIFF_WIDTH), lambda b, i: (b, l, i, 0))] * 2
    else:
        in_specs += [layer((DIFF_WIDTH, D_MODEL))]
        args += [lw["w_vt"]]
        out_shape += [sds(256, BF16), jax.ShapeDtypeStruct((bsz, DIFF_WIDTH, n), BF16)]
        out_specs += [tok(256), pl.BlockSpec((1, DIFF_WIDTH, tm), lambda b, i: (b, 0, i))]
    return pl.pallas_call(
        functools.partial(_inproj_kernel, tm=tm, n_tiles=n_tiles, use_rope=use_rope, cache_out=cache_out),
        grid=(bsz, n_tiles),
        in_specs=in_specs,
        out_specs=tuple(out_specs),
        out_shape=tuple(out_shape),
        input_output_aliases=aliases,
        compiler_params=_params("arbitrary", "arbitrary"),
        name="in_proj_rope" if use_rope else "in_proj",
    )(*args)


def _expand_matrix():
    e = np.zeros((256, 3 * 1024), np.float32)
    for part in range(2):
        for qn in range(3):
            for d in range(2):
                for h in range(SSD_HEADS):
                    r = part * 128 + qn * 16 + d * 8 + h
                    c0 = qn * 1024 + d * 512 + h * 64
                    e[r, c0:c0 + 64] = 1.0
    return jnp.asarray(e, dtype=BF16)


def _ssd_kernel(*refs, nc, has_init, emit_state):
    refs = list(refs)
    xs_ref, b_ref, c_ref, dt_ref, z_ref, prm_ref, dexp_ref, nw_ref, emat_ref = refs[:9]
    pos = 9
    if has_init:
        s0f_ref, s0b_ref = refs[pos:pos + 2]
        pos += 2
    if emit_state:
        pos += 2
    y_ref = refs[pos]
    pos += 1
    if emit_state:
        sf_ref, sb_ref = refs[pos:pos + 2]
        pos += 2
    st_scr, gxf_scr, gxb_scr, xdtf_scr, xdtb_scr, xdf_scr, xdb_scr, acum_scr = refs[pos:pos + 8]
    grp = 4 if nc % 4 == 0 else (2 if nc % 2 == 0 else 1)

    lane = lax.broadcasted_iota(jnp.int32, (1, 128), 1)
    ii = lax.broadcasted_iota(jnp.int32, (128, 128), 0)
    jj = lax.broadcasted_iota(jnp.int32, (128, 128), 1)
    low = jj <= ii
    upp = jj >= ii
    ltri = jnp.where(low, 1.0, 0.0).astype(BF16)
    bias = prm_ref[0, 0:1, :]
    a_row = -jnp.exp(prm_ref[0, 1:2, :])
    is_head = lane < 2 * SSD_HEADS
    is_fwd = lane < SSD_HEADS
    half = lane < 64

    for blk in range(4):
        bl = slice(blk * 128, (blk + 1) * 128)
        if has_init:
            st_scr[0, :, bl] = s0f_ref[0, 0, bl, :].T
            st_scr[1, :, bl] = s0b_ref[0, 0, bl, :].T
        else:
            st_scr[0, :, bl] = jnp.zeros((SSD_STATE, 128), F32)
            st_scr[1, :, bl] = jnp.zeros((SSD_STATE, 128), F32)

    def prep_body(t, carry):
        r0 = pl.multiple_of(t * (grp * SSD_CHUNK), grp * SSD_CHUNK)
        rows = pl.ds(r0, grp * SSD_CHUNK)
        dtv = jnp.where(is_head, jax.nn.softplus(dt_ref[0, rows, :] + bias), 0.0)
        dta = dtv * a_row
        chunk = lambda v, c: v[c * SSD_CHUNK:(c + 1) * SSD_CHUNK]
        parts = []
        for c in range(grp):
            parts += list(_split3(chunk(dta, c)))
        a3 = jnp.dot(ltri, jnp.concatenate(parts, axis=1), preferred_element_type=F32)
        acums, packs = [], []
        for c in range(grp):
            o = c * 384
            cum = a3[:, o:o + 128] + a3[:, o + 128:o + 256] + a3[:, o + 256:o + 384]
            total = cum[127:128, :]
            acum = jnp.where(is_fwd, cum, total - cum + chunk(dta, c))
            gdec = jnp.where(is_head, jnp.exp(acum), 0.0)
            wdec = chunk(dtv, c) * jnp.exp(total - acum)
            acums.append(acum)
            packs.append(chunk(dtv, c) + pltpu.roll(wdec, 16, 1) + pltpu.roll(gdec, 32, 1))
        packed = jnp.concatenate(packs, axis=0)
        x6 = jnp.dot(jnp.concatenate(_split2(packed), axis=1), emat_ref[...], preferred_element_type=F32)
        x = xs_ref[0, rows, :]
        xdtf_scr[rows, :] = (x * x6[:, 0:512]).astype(BF16)
        xdtb_scr[rows, :] = (x * x6[:, 512:1024]).astype(BF16)
        xdf_scr[rows, :] = (x * x6[:, 1024:1536]).astype(BF16)
        xdb_scr[rows, :] = (x * x6[:, 1536:2048]).astype(BF16)
        gxf_scr[rows, :] = x6[:, 2048:2560]
        gxb_scr[rows, :] = x6[:, 2560:3072]
        acum_scr[rows, :] = jnp.concatenate(acums, axis=0)
        return carry

    lax.fori_loop(0, nc // grp, prep_body, 0, unroll=2 if (nc // grp) % 2 == 0 else 1)

    def fwd_body(c, carry):
        r0 = pl.multiple_of(c * SSD_CHUNK, SSD_CHUNK)
        rows = pl.ds(r0, SSD_CHUNK)
        acum = acum_scr[rows, :]
        acum_t = acum.T
        gx_f = gxf_scr[rows, :]
        xdt_f = xdtf_scr[rows, :]
        xdt_b = xdtb_scr[rows, :]
        xd_f = xdf_scr[rows, :]
        bc = b_ref[0, rows, :]
        cc = c_ref[0, rows, :]

        y_parts = []
        for g in range(2):
            gl = slice(g * 256, (g + 1) * 256)
            bg = bc[:, g * 128:(g + 1) * 128]
            cg = cc[:, g * 128:(g + 1) * 128]
            cb = lax.dot_general(cg, bg, _NT, preferred_element_type=F32)
            s_f = st_scr[0, :, gl]
            yo_f = jnp.dot(cg, s_f.astype(BF16), preferred_element_type=F32) * gx_f[:, gl]
            st_scr[0, :, gl] = s_f * gx_f[127:128, gl] + lax.dot_general(
                bg, xd_f[:, gl], _TN, preferred_element_type=F32)
            for pair in range(2):
                hp = 2 * g + pair
                pl_ = slice(hp * 128, (hp + 1) * 128)
                rhs = jnp.concatenate([xdt_f[:, pl_], xdt_b[:, pl_]], axis=0)
                ys = []
                for sub in range(2):
                    h = 2 * hp + sub
                    hb = SSD_HEADS + h
                    seg_f = acum[:, h:h + 1] - acum_t[h:h + 1, :]
                    m_f = jnp.exp(jnp.where(low, seg_f, NEG_BIG)) * cb
                    seg_b = acum[:, hb:hb + 1] - acum_t[hb:hb + 1, :]
                    m_b = jnp.exp(jnp.where(upp, seg_b, NEG_BIG)) * cb
                    lhs = jnp.concatenate([m_f, m_b], axis=1).astype(BF16)
                    ys.append(jnp.dot(lhs, rhs, preferred_element_type=F32))
                y_pair = jnp.where(half, ys[0], ys[1])
                y_parts.append(y_pair + yo_f[:, pair * 128:(pair + 1) * 128])
        y_ref[0, rows, :] = jnp.concatenate(y_parts, axis=1)
        return carry

    lax.fori_loop(0, nc, fwd_body, 0, unroll=4 if nc % 4 == 0 else 2)
    if emit_state:
        for blk in range(4):
            bl = slice(blk * 128, (blk + 1) * 128)
            sf_ref[0, 0, bl, :] = st_scr[0, :, bl].T

    dexp = dexp_ref[0]
    nw = nw_ref[0]

    def bwd_body(t, carry):
        c = nc - 1 - t
        r0 = pl.multiple_of(c * SSD_CHUNK, SSD_CHUNK)
        rows = pl.ds(r0, SSD_CHUNK)
        bc = b_ref[0, rows, :]
        cc = c_ref[0, rows, :]
        gx_b = gxb_scr[rows, :]
        xd_b = xdb_scr[rows, :]
        yo = []
        for g in range(2):
            gl = slice(g * 256, (g + 1) * 256)
            bg = bc[:, g * 128:(g + 1) * 128]
            cg = cc[:, g * 128:(g + 1) * 128]
            s_b = st_scr[1, :, gl]
            yo.append(jnp.dot(cg, s_b.astype(BF16), preferred_element_type=F32) * gx_b[:, gl])
            st_scr[1, :, gl] = s_b * gx_b[0:1, gl] + lax.dot_general(
                bg, xd_b[:, gl], _TN, preferred_element_type=F32)
        x = xs_ref[0, rows, :]
        y = y_ref[0, rows, :] + jnp.concatenate(yo, axis=1) + dexp * x
        y = y * _silu(z_ref[0, rows, :])
        ms = jnp.mean(y * y, axis=-1, keepdims=True)
        y_ref[0, rows, :] = y * lax.rsqrt(ms + RMS_EPS) * nw
        return carry

    lax.fori_loop(0, nc, bwd_body, 0, unroll=4 if nc % 4 == 0 else 2)
    if emit_state:
        for blk in range(4):
            bl = slice(blk * 128, (blk + 1) * 128)
            sb_ref[0, 0, bl, :] = st_scr[1, :, bl].T


def _ssd_call(xs, bm, cm, dt, z, lw, l, init, state_bufs):
    bsz, n, _ = xs.shape
    nc = n // SSD_CHUNK
    seq = lambda w: pl.BlockSpec((1, n, w), lambda b: (b, 0, 0))
    st = pl.BlockSpec((1, 1, SSD_INNER, SSD_STATE), lambda b: (b, l, 0, 0))
    layer = lambda shp: pl.BlockSpec((1,) + shp, lambda b: (l,) + (0,) * len(shp))
    in_specs = [seq(512), seq(256), seq(256), seq(128), seq(512),
                layer((8, 128)), layer((1, 512)), layer((1, 512)),
                pl.BlockSpec((256, 3072), lambda b: (0, 0))]
    args = [xs, bm, cm, dt, z, lw["prm"], lw["dexp"], lw["ssd_nw"], lw["emat"]]
    has_init = init is not None
    emit_state = state_bufs is not None
    if has_init:
        in_specs += [st, st]
        args += list(init)
    out_shape = [jax.ShapeDtypeStruct((bsz, n, SSD_INNER), F32)]
    out_specs = [seq(512)]
    aliases = {}
    if emit_state:
        aliases = {len(args): 1, len(args) + 1: 2}
        in_specs += [pl.BlockSpec(memory_space=pl.ANY)] * 2
        args += list(state_bufs)
        out_shape += [jax.ShapeDtypeStruct(state_bufs[0].shape, F32)] * 2
        out_specs += [st, st]
    return pl.pallas_call(
        functools.partial(_ssd_kernel, nc=nc, has_init=has_init, emit_state=emit_state),
        grid=(bsz,),
        in_specs=in_specs,
        out_specs=tuple(out_specs),
        out_shape=tuple(out_shape),
        input_output_aliases=aliases,
        scratch_shapes=[pltpu.VMEM((2, SSD_STATE, SSD_INNER), F32)]
                       + [pltpu.VMEM((n, SSD_INNER), F32)] * 2
                       + [pltpu.VMEM((n, SSD_INNER), BF16)] * 4
                       + [pltpu.VMEM((n, 128), F32)],
        compiler_params=_params("arbitrary"),
        name="ssd_scan_init" if has_init else "ssd_scan",
    )(*args)


def _attn_kernel(*refs, lam_init, has_prefix, from_cache):
    refs = list(refs)
    q_ref, k_ref, v_ref = refs[:3]
    pos = 3
    if has_prefix:
        kp_ref, vp_ref = refs[pos:pos + 2]
        pos += 2
    lp_ref, nw_ref, ones_ref, o_ref = refs[pos:pos + 4]

    q = q_ref[0]
    tq = q.shape[0]
    if from_cache:
        segs = [(k_ref[0, 0].astype(BF16), v_ref[0, 0].T.astype(BF16))]
    else:
        segs = [(k_ref[0], v_ref[0])]
    if has_prefix:
        segs.append((kp_ref[0, 0].T.astype(BF16), vp_ref[0, 0].astype(BF16)))
    lp = lp_ref[0]
    lam = (jnp.exp(jnp.sum(lp[0:1] * lp[1:2], axis=-1, keepdims=True))
           - jnp.exp(jnp.sum(lp[2:3] * lp[3:4], axis=-1, keepdims=True)) + lam_init)
    lane = lax.broadcasted_iota(jnp.int32, (1, 128), 1)
    row = lax.broadcasted_iota(jnp.int32, (128, 1), 0)

    def scores(head):
        g, hh = divmod(head, 2)
        gs = slice(g * 128, (g + 1) * 128)
        qg = q[:, gs]
        qms = []
        for s in range(2):
            lo = (2 * hh + s) * DIFF_HD
            qms.append(jnp.where((lane >= lo) & (lane < lo + DIFF_HD), qg, jnp.zeros_like(qg)))
        qm = jnp.concatenate(qms, axis=0)
        return [lax.dot_general(kk[:, gs], qm, _NT, preferred_element_type=F32) for kk, _ in segs]

    def weighted_values(head, scs):
        g, hh = divmod(head, 2)
        own = slice(DIFF_VD * hh, DIFF_VD * (hh + 1))
        other = DIFF_VD * (1 - hh)
        vmod = []
        for _, vt in segs:
            mine = vt[g * 128 + DIFF_VD * hh:g * 128 + DIFF_VD * (hh + 1), :]
            ones = jnp.ones_like(mine)
            vmod.append(jnp.concatenate([mine, ones] if hh == 0 else [ones, mine], axis=0))
        mx = jnp.max(scs[0], axis=0, keepdims=True)
        for sc in scs[1:]:
            mx = jnp.maximum(mx, jnp.max(sc, axis=0, keepdims=True))
        res = None
        for sc, vm in zip(scs, vmod):
            part = jnp.dot(vm, jnp.exp2(sc - mx).astype(BF16), preferred_element_type=F32)
            res = part if res is None else res + part
        r0, r1 = res[:, :tq], res[:, tq:]
        return r0[own] * (1.0 / r0[other:other + 1]) - r1[own] * (lam / r1[other:other + 1])

    o_heads = []
    pending = scores(0)
    for head in range(DIFF_HEADS):
        nxt = scores(head + 1) if head + 1 < DIFF_HEADS else None
        o_heads.append(weighted_values(head, pending))
        pending = nxt
    o = jnp.concatenate(o_heads, axis=0).T
    ms = jnp.dot(jnp.concatenate(_split2(o * o), axis=1), ones_ref[...],
                 preferred_element_type=F32) * (1.0 / DIFF_VD)
    o_ref[0] = o * lax.rsqrt(ms + RMS_EPS) * nw_ref[0] * (1.0 - lam_init)


def _head_ones():
    e = np.zeros((512, 256), np.float32)
    for part in range(2):
        for h in range(DIFF_HEADS):
            e[part * 256 + h * 64:part * 256 + (h + 1) * 64, h * 64:(h + 1) * 64] = 1.0
    return jnp.asarray(e, dtype=BF16)


def _attn_call(q, k, v, prefix, lw, l, lam_init, from_cache, tq=256):
    bsz, n, _ = q.shape
    layer = lambda shp: pl.BlockSpec((1,) + shp, lambda b, i: (l,) + (0,) * len(shp))
    if from_cache:
        kv_spec = pl.BlockSpec((1, 1, n, DIFF_WIDTH), lambda b, i: (b, l, 0, 0))
    else:
        kv_spec = pl.BlockSpec((1, n, DIFF_WIDTH), lambda b, i: (b, 0, 0))
    v_spec = kv_spec if from_cache else pl.BlockSpec((1, DIFF_WIDTH, n), lambda b, i: (b, 0, 0))
    in_specs = [pl.BlockSpec((1, tq, DIFF_WIDTH), lambda b, i: (b, i, 0)), kv_spec, v_spec]
    args = [q, k, v]
    has_prefix = prefix is not None
    if has_prefix:
        past = prefix[0].shape[3]
        in_specs += [pl.BlockSpec((1, 1, DIFF_WIDTH, past), lambda b, i: (b, l, 0, 0))] * 2
        args += list(prefix)
    in_specs += [layer((4, DIFF_HD)), layer((1, DIFF_WIDTH)), pl.BlockSpec((512, 256), lambda b, i: (0, 0))]
    args += [lw["lp"], lw["attn_nw"], lw["ones"]]
    return pl.pallas_call(
        functools.partial(_attn_kernel, lam_init=lam_init, has_prefix=has_prefix, from_cache=from_cache),
        grid=(bsz, n // tq),
        in_specs=in_specs,
        out_specs=pl.BlockSpec((1, tq, DIFF_WIDTH), lambda b, i: (b, i, 0)),
        out_shape=jax.ShapeDtypeStruct((bsz, n, DIFF_WIDTH), F32),
        compiler_params=_params("arbitrary", "arbitrary"),
        name="diff_attn_prefix" if has_prefix else "diff_attn",
    )(*args)


def _mlp_kernel(x_ref, yssd_ref, ysc_ref, yat_ref, mod_ref, wo_ref, wu_ref, wd_ref, ln_ref, o_ref, *, parts):
    g1 = mod_ref[0, 2:3, :]
    sh2 = mod_ref[0, 3:4, :]
    sc2 = mod_ref[0, 4:5, :]
    g2 = mod_ref[0, 5:6, :]
    rows_per = x_ref.shape[1] // parts

    def mix_norm(r):
        rows = slice(r * rows_per, (r + 1) * rows_per)
        m = (jnp.dot(yssd_ref[0, rows, :].astype(BF16), wo_ref[0, 0:512, :], preferred_element_type=F32)
             + jnp.dot(ysc_ref[0, rows, :].astype(BF16), wo_ref[0, 512:768, :], preferred_element_type=F32)
             + jnp.dot(yat_ref[0, rows, :].astype(BF16), wo_ref[0, 768:1024, :], preferred_element_type=F32))
        x1 = _layer_norm(ALPHA * x_ref[0, rows, :] + g1 * m, ln_ref[0, 0:1, :], ln_ref[0, 1:2, :])
        return x1, (x1 * (1.0 + sc2) + sh2).astype(BF16)

    def mlp_norm(r, x1, h):
        rows = slice(r * rows_per, (r + 1) * rows_per)
        up = jnp.dot(h, wu_ref[0], preferred_element_type=F32)
        act = jnp.square(jnp.maximum(up, 0.0)).astype(BF16)
        f = jnp.dot(act, wd_ref[0], preferred_element_type=F32)
        o_ref[0, rows, :] = _layer_norm(ALPHA * x1 + g2 * f, ln_ref[0, 2:3, :], ln_ref[0, 3:4, :])

    staged = mix_norm(0)
    for r in range(parts):
        nxt = mix_norm(r + 1) if r + 1 < parts else None
        mlp_norm(r, *staged)
        staged = nxt


def _mlp_call(x, yssd, ysc, yat, mod, lw, l, tm):
    bsz, n, _ = x.shape
    bm = mod.shape[0]
    mod_idx = (lambda b, i: (b, 0, 0)) if bm > 1 else (lambda b, i: (0, 0, 0))
    tok = lambda w: pl.BlockSpec((1, tm, w), lambda b, i: (b, i, 0))
    layer = lambda shp: pl.BlockSpec((1,) + shp, lambda b, i: (l,) + (0,) * len(shp),
                                     pipeline_mode=pl.Buffered(1))
    return pl.pallas_call(
        functools.partial(_mlp_kernel, parts=tm // 256),
        grid=(bsz, n // tm),
        in_specs=[tok(D_MODEL), tok(512), tok(256), tok(256),
                  pl.BlockSpec((1, 6, D_MODEL), mod_idx),
                  layer((D_MODEL, D_MODEL)), layer((D_MODEL, D_FF)), layer((D_FF, D_MODEL)),
                  layer((4, D_MODEL))],
        out_specs=tok(D_MODEL),
        out_shape=jax.ShapeDtypeStruct((bsz, n, D_MODEL), F32),
        compiler_params=_params("arbitrary", "arbitrary"),
        name="out_mlp",
    )(x, yssd, ysc, yat, mod, lw["w_out"], lw["w_up"], lw["w_down"], lw["ln"])


def _rope_tables(n):
    rows = n // GRID_W
    row = jnp.repeat(jnp.arange(rows, dtype=F32), GRID_W)
    col = jnp.tile(jnp.arange(GRID_W, dtype=F32), rows)
    inv = ROPE_BASE ** (-jnp.arange(ROPE_F, dtype=F32) / ROPE_F)
    ang = jnp.stack([row[:, None] * inv, col[:, None] * inv], axis=1)
    ang = jnp.repeat(ang[:, :, None, :], 2, axis=2).reshape(n, DIFF_HD)
    sign = jnp.tile(jnp.concatenate([-jnp.ones(ROPE_F, F32), jnp.ones(ROPE_F, F32)]), 2)
    cos_t = jnp.tile(jnp.cos(ang), (1, 2 * DIFF_HEADS))
    sin_t = jnp.tile(jnp.sin(ang) * sign, (1, 2 * DIFF_HEADS))
    return cos_t, sin_t


def _layer_weights(w_in, ssd_conv_w, ssd_conv_b, ssd_dt_bias, ssd_a_log, ssd_d, ssd_norm_w, sc_conv_w,
                   diff_lambda, diff_norm_w, w_out, ln1_g, ln1_b, w_up, w_down, ln2_g, ln2_b):
    w_in_p, w_vt = _permute_w_in(jnp.swapaxes(w_in, 1, 2))
    prm = jnp.zeros((DEPTH, 8, 128), F32)
    prm = prm.at[:, 0, :16].set(ssd_dt_bias.reshape(DEPTH, 16)).at[:, 1, :16].set(ssd_a_log.reshape(DEPTH, 16))
    return {
        "w_in": w_in_p,
        "w_vt": w_vt,
        "conv_w": ssd_conv_w,
        "conv_b": ssd_conv_b.reshape(DEPTH, 1, SSD_CONV_CH),
        "sc_w": sc_conv_w,
        "prm": prm,
        "dexp": jnp.repeat(ssd_d, SSD_HEAD_DIM, axis=1).reshape(DEPTH, 1, SSD_INNER),
        "ssd_nw": ssd_norm_w.reshape(DEPTH, 1, SSD_INNER),
        "emat": _expand_matrix(),
        "lp": diff_lambda,
        "attn_nw": jnp.tile(diff_norm_w, (1, DIFF_HEADS)).reshape(DEPTH, 1, DIFF_WIDTH),
        "ones": _head_ones(),
        "w_out": w_out.astype(BF16),
        "w_up": w_up.astype(BF16),
        "w_down": w_down.astype(BF16),
        "ln": jnp.stack([ln1_g, ln1_b, ln2_g, ln2_b], axis=1),
    }


def _trunk_layer(x, mod, lw, l, lam_init, rope, prefix, init, caches, state_bufs, tm):
    bsz, n, _ = x.shape
    res = _inproj_call(x, mod, lw, l, rope, caches, tm)
    z, xs, bm, cm, dt, ysc, q, k, v = res
    ssd = _ssd_call(xs, bm, cm, dt, z, lw, l, init, state_bufs)
    yat = _attn_call(q, k, v, prefix, lw, l, lam_init, from_cache=caches is not None, tq=min(n, 512))
    if mod.shape[0] == 1:
        flat = lambda t: t.reshape(1, bsz * n, t.shape[-1])
        y = _mlp_call(flat(x), flat(ssd[0]), flat(ysc), flat(yat), mod, lw, l, 1024).reshape(bsz, n, D_MODEL)
    else:
        y = _mlp_call(x, ssd[0], ysc, yat, mod, lw, l, 1024)
    new_caches = (k, v) if caches is not None else None
    new_states = (ssd[1], ssd[2]) if state_bufs is not None else None
    return y, new_caches, new_states


def kernel(x_prompt, x_sample, cache_k, cache_v, state_ssm_fwd, state_ssm_bwd, c, c_ctx, w_mod, b_mod, w_in,
           ssd_conv_w, ssd_conv_b, ssd_dt_bias, ssd_a_log, ssd_d, ssd_norm_w, sc_conv_w, diff_lambda,
           diff_norm_w, w_out, ln1_g, ln1_b, w_up, w_down, ln2_g, ln2_b):
    bp, seq, _ = x_prompt.shape
    bs, dec_seq, _ = x_sample.shape
    past = cache_k.shape[2]

    cvec = jnp.zeros((16, D_MODEL), F32).at[:bs].set(c).at[bs].set(c_ctx)
    mods = _mod_call(cvec, w_mod, b_mod)
    rope = _rope_tables(dec_seq)
    lw = _layer_weights(w_in, ssd_conv_w, ssd_conv_b, ssd_dt_bias, ssd_a_log, ssd_d, ssd_norm_w, sc_conv_w,
                        diff_lambda, diff_norm_w, w_out, ln1_g, ln1_b, w_up, w_down, ln2_g, ln2_b)
    prefix = (cache_k.transpose(0, 1, 3, 4, 2).reshape(bs, DEPTH, DIFF_WIDTH, past),
              cache_v.transpose(0, 1, 3, 4, 2).reshape(bs, DEPTH, DIFF_WIDTH, past))
    init = (state_ssm_fwd.reshape(bs, DEPTH, SSD_INNER, SSD_STATE),
            state_ssm_bwd.reshape(bs, DEPTH, SSD_INNER, SSD_STATE))
    caches = (jnp.zeros((bp, DEPTH, seq, DIFF_WIDTH), F32), jnp.zeros((bp, DEPTH, seq, DIFF_WIDTH), F32))
    states = (jnp.zeros((bp, DEPTH, SSD_INNER, SSD_STATE), F32), jnp.zeros((bp, DEPTH, SSD_INNER, SSD_STATE), F32))

    hp, hs = x_prompt, x_sample
    for l in range(DEPTH):
        lam_init = 0.8 - 0.6 * math.exp(-0.3 * l)
        mod_ctx = mods[l, bs:bs + 1].reshape(1, 6, D_MODEL)
        mod_lat = mods[l, :bs].reshape(bs, 6, D_MODEL)
        hp, caches, states = _trunk_layer(hp, mod_ctx, lw, l, lam_init, None, None, None, caches, states, seq)
        hs = _trunk_layer(hs, mod_lat, lw, l, lam_init, rope, prefix, init, None, None, 1024)[0]

    return (hp, hs,
            caches[0].reshape(bp, DEPTH, seq, 2 * DIFF_HEADS, DIFF_HD),
            caches[1].reshape(bp, DEPTH, seq, DIFF_HEADS, DIFF_VD),
            states[0].reshape(bp, DEPTH, SSD_HEADS, SSD_HEAD_DIM, SSD_STATE),
            states[1].reshape(bp, DEPTH, SSD_HEADS, SSD_HEAD_DIM, SSD_STATE))
```
